```python
import jax, jax.numpy as jnp
from jax import lax
import numpy as np


D_MODEL = 1024
BATCH = 4
SEQ = 4096
DEPTH = 4

GRID_W = 64
CTX_LEN = 256
N_MIXERS = 3
EPS = 1e-6
NEG_INF = -1e30
N_HEADS = 16
N_KV_HEADS = 4
HEAD_DIM = 64
KV_GROUP = N_HEADS // N_KV_HEADS
WINDOW = 128
ATTN_BLOCK = 128
ROPE_THETA = 10000.0
QKV_WIDTH = (N_HEADS + 2 * N_KV_HEADS) * HEAD_DIM
D_RNN = D_MODEL
N_LRU_BLOCKS = 8
LRU_BLOCK = D_RNN // N_LRU_BLOCKS
LRU_CONV = 4
LRU_CONV_LEFT = 2
LRU_C = 8.0
CONF_KERNEL = 31
N_KEYS = 128
N_EXPERTS = N_KEYS * N_KEYS
PEER_HEADS = 8
PEER_QDIM = 256
PEER_TOPK = 16
PEER_CHUNK = 128

N_ATTN_LAYERS = (DEPTH + 2) // 3
N_LRU_LAYERS = (DEPTH + 1) // 3
N_CONV_LAYERS = DEPTH // 3

kernel_name = 'hybrid_diffusion_trunk'


def rms_norm(x, g):
    xf = x.astype(jnp.float32)
    y = xf * lax.rsqrt(jnp.mean(xf * xf, axis=-1, keepdims=True) + EPS)
    return (y * g.astype(jnp.float32)).astype(x.dtype)


def layer_norm(x, g, b):
    xf = x.astype(jnp.float32)
    mu = jnp.mean(xf, axis=-1, keepdims=True)
    var = jnp.mean(jnp.square(xf - mu), axis=-1, keepdims=True)
    y = (xf - mu) * lax.rsqrt(var + EPS)
    return (y * g.astype(jnp.float32) + b.astype(jnp.float32)).astype(x.dtype)


def modulate(h, shift, scale):
    return h * (1.0 + scale) + shift


def axial_rope_tables(n_tokens):
    rows = n_tokens // GRID_W
    row = jnp.repeat(jnp.arange(rows), GRID_W).astype(jnp.float32)
    col = jnp.tile(jnp.arange(GRID_W), rows).astype(jnp.float32)
    n_freq = HEAD_DIM // 4
    freqs = ROPE_THETA ** (-jnp.arange(n_freq, dtype=jnp.float32) / n_freq)
    ang = jnp.stack([row[:, None] * freqs, col[:, None] * freqs], axis=1)
    return jnp.cos(ang), jnp.sin(ang)


def apply_axial_rope(x, cos, sin):
    b, l, h, d = x.shape
    xr = x.reshape(b, l, h, 2, 2, d // 4)
    x1, x2 = xr[..., 0, :], xr[..., 1, :]
    cs = cos[None, :, None].astype(x.dtype)
    sn = sin[None, :, None].astype(x.dtype)
    out = jnp.stack([x1 * cs - x2 * sn, x2 * cs + x1 * sn], axis=-2)
    return out.reshape(b, l, h, d)


def depthwise_conv(x, w, bias, left):
    k = w.shape[0]
    ch = x.shape[-1]
    xp = jnp.pad(x, ((0, 0), (left, k - 1 - left), (0, 0)))
    y = lax.conv_general_dilated(xp, w.astype(x.dtype)[:, None, :], window_strides=(1,), padding='VALID',
                                 dimension_numbers=('NWC', 'WIO', 'NWC'), feature_group_count=ch)
    return y + bias


def windowed_gqa_mixer(hc, hl, w_qkv, q_gain, k_gain, sink, w_o, cos, sin, need_ctx):
    def project(h):
        b, l, _ = h.shape
        q, k, v = jnp.split(h @ w_qkv, [N_HEADS * HEAD_DIM, (N_HEADS + N_KV_HEADS) * HEAD_DIM], axis=-1)
        q = rms_norm(q.reshape(b, l, N_HEADS, HEAD_DIM), q_gain)
        k = rms_norm(k.reshape(b, l, N_KV_HEADS, HEAD_DIM), k_gain)
        return q, k, v.reshape(b, l, N_KV_HEADS, HEAD_DIM)

    qc, kc, vc = project(hc)
    ql, kl, vl = project(hl)
    ql = apply_axial_rope(ql, cos, sin)
    kl = apply_axial_rope(kl, cos, sin)
    b, l = hl.shape[:2]
    lc = hc.shape[1]
    scale = HEAD_DIM ** -0.5
    sink_f = sink.astype(jnp.float32).reshape(N_KV_HEADS, KV_GROUP)
    ql = ql.reshape(b, l, N_KV_HEADS, KV_GROUP, HEAD_DIM)
    pad = ((0, 0), (ATTN_BLOCK, ATTN_BLOCK), (0, 0), (0, 0))
    kl_pad = jnp.pad(kl, pad)
    vl_pad = jnp.pad(vl, pad)
    n_blocks = l // ATTN_BLOCK

    def block(bi):
        start = bi * ATTN_BLOCK
        qb = lax.dynamic_slice_in_dim(ql, start, ATTN_BLOCK, axis=1)
        kb = lax.dynamic_slice_in_dim(kl_pad, start, 3 * ATTN_BLOCK, axis=1)
        vb = lax.dynamic_slice_in_dim(vl_pad, start, 3 * ATTN_BLOCK, axis=1)
        s_loc = jnp.einsum('bqhgd,bkhd->bhgqk', qb, kb).astype(jnp.float32) * scale
        qpos = start + jnp.arange(ATTN_BLOCK)
        kpos = start - ATTN_BLOCK + jnp.arange(3 * ATTN_BLOCK)
        valid = (jnp.abs(qpos[:, None] - kpos[None, :]) <= WINDOW) & (kpos >= 0)[None, :] & (kpos < l)[None, :]
        s_loc = jnp.where(valid, s_loc, NEG_INF)
        s_ctx = jnp.einsum('bqhgd,bkhd->bhgqk', qb, kc).astype(jnp.float32) * scale
        s_sink = jnp.broadcast_to(sink_f[None, :, :, None, None], (b, N_KV_HEADS, KV_GROUP, ATTN_BLOCK, 1))
        p = jax.nn.softmax(jnp.concatenate([s_loc, s_ctx, s_sink], axis=-1), axis=-1)
        p_loc = p[..., :3 * ATTN_BLOCK].astype(vl.dtype)
        p_ctx = p[..., 3 * ATTN_BLOCK:3 * ATTN_BLOCK + lc].astype(vc.dtype)
        return (jnp.einsum('bhgqk,bkhd->bqhgd', p_loc, vb)
                + jnp.einsum('bhgqk,bkhd->bqhgd', p_ctx, vc))

    ol = lax.map(block, jnp.arange(n_blocks))
    yl = jnp.moveaxis(ol, 0, 1).reshape(b, l, N_HEADS * HEAD_DIM) @ w_o
    yc = None
    if need_ctx:
        qc = qc.reshape(b, lc, N_KV_HEADS, KV_GROUP, HEAD_DIM)
        s = jnp.einsum('bqhgd,bkhd->bhgqk', qc, kc).astype(jnp.float32) * scale
        s_sink = jnp.broadcast_to(sink_f[None, :, :, None, None], (b, N_KV_HEADS, KV_GROUP, lc, 1))
        p = jax.nn.softmax(jnp.concatenate([s, s_sink], axis=-1), axis=-1)
        oc = jnp.einsum('bhgqk,bkhd->bqhgd', p[..., :lc].astype(vc.dtype), vc)
        yc = oc.reshape(b, lc, N_HEADS * HEAD_DIM) @ w_o
    return yc, yl


def block_diag_linear(x, w, bias):
    xs = x.reshape(x.shape[:-1] + (N_LRU_BLOCKS, LRU_BLOCK))
    return jnp.einsum('...ni,nij->...nj', xs, w).reshape(x.shape) + bias


def lru_coeffs(u, w_a, b_a, w_x, b_x, lam):
    uf = u.astype(jnp.float32)
    r = jax.nn.sigmoid(block_diag_linear(uf, w_a.astype(jnp.float32), b_a.astype(jnp.float32)))
    i = jax.nn.sigmoid(block_diag_linear(uf, w_x.astype(jnp.float32), b_x.astype(jnp.float32)))
    log_a = -LRU_C * r * jax.nn.softplus(-lam.astype(jnp.float32))
    a = jnp.exp(log_a)
    bterm = jnp.sqrt(-jnp.expm1(2.0 * log_a)) * (i * uf)
    return a, bterm


def linear_scan(a, bterm, h0):
    def combine(lft, rgt):
        return lft[0] * rgt[0], rgt[0] * lft[1] + rgt[1]
    a_cum, h = lax.associative_scan(combine, (a, bterm), axis=1)
    return h + a_cum * h0[:, None, :]


def maybe_flip(t, rev):
    return t[:, ::-1] if rev else t


def rglru_mixer(hc, hl, w_in, conv_w, conv_b, w_a, b_a, w_x, b_x, lam, w_out, need_ctx):
    def branches(h):
        gate, u = jnp.split(h @ w_in, 2, axis=-1)
        return jax.nn.gelu(gate), depthwise_conv(u, conv_w, conv_b, LRU_CONV_LEFT)

    gate_c, u_c = branches(hc)
    gate_l, u_l = branches(hl)
    h0 = jnp.zeros((hc.shape[0], D_RNN), jnp.float32)
    dirs_c, dirs_l = [], []
    for d in range(2):
        rev = d == 1
        a_c, b_c = lru_coeffs(maybe_flip(u_c, rev), w_a[d], b_a[d], w_x[d], b_x[d], lam[d])
        h_c = linear_scan(a_c, b_c, h0)
        a_l, b_l = lru_coeffs(maybe_flip(u_l, rev), w_a[d], b_a[d], w_x[d], b_x[d], lam[d])
        h_l = linear_scan(a_l, b_l, h_c[:, -1])
        dirs_c.append(maybe_flip(h_c, rev))
        dirs_l.append(maybe_flip(h_l, rev))
    yl = ((dirs_l[0] + dirs_l[1]).astype(hl.dtype) * gate_l) @ w_out
    yc = ((dirs_c[0] + dirs_c[1]).astype(hc.dtype) * gate_c) @ w_out if need_ctx else None
    return yc, yl


def conformer_conv_mixer(hc, hl, w_pw1, b_pw1, dw_w, dw_b, ln_g, ln_b, w_pw2, b_pw2, need_ctx):
    def conv_module(h):
        val, gate = jnp.split(h @ w_pw1 + b_pw1, 2, axis=-1)
        u = depthwise_conv(val * jax.nn.sigmoid(gate), dw_w, dw_b, CONF_KERNEL // 2)
        u = jax.nn.silu(layer_norm(u, ln_g, ln_b))
        return u @ w_pw2 + b_pw2
    yc = conv_module(hc) if need_ctx else None
    return yc, conv_module(hl)


def peer_ffn(h, w_q, keys1, keys2, u_tab, v_tab):
    b, l, d = h.shape
    q = (h @ w_q).reshape(b, l, PEER_HEADS, 2, PEER_QDIM // 2)
    s1 = jnp.einsum('blhd,hkd->blhk', q[..., 0, :], keys1).astype(jnp.float32)
    s2 = jnp.einsum('blhd,hkd->blhk', q[..., 1, :], keys2).astype(jnp.float32)
    v1, i1 = lax.top_k(s1, PEER_TOPK)
    v2, i2 = lax.top_k(s2, PEER_TOPK)
    n_cand = PEER_TOPK * PEER_TOPK
    cand = (v1[..., :, None] + v2[..., None, :]).reshape(b, l, PEER_HEADS, n_cand)
    cand_idx = (i1[..., :, None] * N_KEYS + i2[..., None, :]).reshape(b, l, PEER_HEADS, n_cand)
    top_s, pos = lax.top_k(cand, PEER_TOPK)
    expert = jnp.take_along_axis(cand_idx, pos, axis=-1)
    gate = jax.nn.softmax(top_s, axis=-1).astype(h.dtype)
    n_chunks = (b * l) // PEER_CHUNK
    hx = h.reshape(n_chunks, PEER_CHUNK, d)
    ex = expert.reshape(n_chunks, PEER_CHUNK, PEER_HEADS, PEER_TOPK)
    gx = gate.reshape(n_chunks, PEER_CHUNK, PEER_HEADS, PEER_TOPK)

    def chunk(args):
        xt, e, g = args
        u = u_tab[e]
        v = v_tab[e]
        act = jax.nn.gelu(jnp.einsum('cd,chkd->chk', xt, u))
        return jnp.einsum('chk,chkd->cd', g * act, v)

    return lax.map(chunk, (hx, ex, gx)).reshape(b, l, d)


def setup_inputs(seed: int = 0) -> dict:
    key = jax.random.key(seed)
    ks = iter(jax.random.split(key, 48))
    f32 = jnp.float32

    def nrm(shape, scale):
        return jax.random.normal(next(ks), shape, dtype=f32) * scale

    def gain(shape):
        return 1.0 + nrm(shape, 0.02)

    u_lam = jax.random.uniform(next(ks), (N_LRU_LAYERS, 2, D_RNN), dtype=f32, minval=0.9, maxval=0.999)
    a_lam = u_lam ** (1.0 / LRU_C)
    lam = jnp.log(a_lam) - jnp.log1p(-a_lam)
    return {
        'x': nrm((BATCH, SEQ, D_MODEL), 1.0),
        'c': nrm((BATCH, D_MODEL), 1.0),
        'ctx': nrm((BATCH, CTX_LEN, D_MODEL), 1.0),
        'c_ctx': nrm((D_MODEL,), 1.0),
        'ada_w': nrm((DEPTH, D_MODEL, 6 * D_MODEL), 0.5 * D_MODEL ** -0.5),
        'ada_b': nrm((DEPTH, 6 * D_MODEL), 0.01),
        'norm_g': gain((DEPTH, 2, D_MODEL)),
        'attn_wqkv': nrm((N_ATTN_LAYERS, D_MODEL, QKV_WIDTH), D_MODEL ** -0.5),
        'attn_q_gain': gain((N_ATTN_LAYERS, HEAD_DIM)),
        'attn_k_gain': gain((N_ATTN_LAYERS, HEAD_DIM)),
        'attn_sink': nrm((N_ATTN_LAYERS, N_HEADS), 0.5),
        'attn_wo': nrm((N_ATTN_LAYERS, N_HEADS * HEAD_DIM, D_MODEL), (N_HEADS * HEAD_DIM) ** -0.5),
        'lru_w_in': nrm((N_LRU_LAYERS, D_MODEL, 2 * D_RNN), D_MODEL ** -0.5),
        'lru_conv_w': nrm((N_LRU_LAYERS, LRU_CONV, D_RNN), LRU_CONV ** -0.5),
        'lru_conv_b': nrm((N_LRU_LAYERS, D_RNN), 0.01),
        'lru_w_a': nrm((N_LRU_LAYERS, 2, N_LRU_BLOCKS, LRU_BLOCK, LRU_BLOCK), LRU_BLOCK ** -0.5),
        'lru_b_a': nrm((N_LRU_LAYERS, 2, D_RNN), 0.01),
        'lru_w_x': nrm((N_LRU_LAYERS, 2, N_LRU_BLOCKS, LRU_BLOCK, LRU_BLOCK), LRU_BLOCK ** -0.5),
        'lru_b_x': nrm((N_LRU_LAYERS, 2, D_RNN), 0.01),
        'lru_lambda': lam,
        'lru_w_out': nrm((N_LRU_LAYERS, D_RNN, D_MODEL), D_RNN ** -0.5),
        'conf_w_pw1': nrm((N_CONV_LAYERS, D_MODEL, 2 * D_MODEL), D_MODEL ** -0.5),
        'conf_b_pw1': nrm((N_CONV_LAYERS, 2 * D_MODEL), 0.01),
        'conf_dw_w': nrm((N_CONV_LAYERS, CONF_KERNEL, D_MODEL), CONF_KERNEL ** -0.5),
        'conf_dw_b': nrm((N_CONV_LAYERS, D_MODEL), 0.01),
        'conf_ln_g': gain((N_CONV_LAYERS, D_MODEL)),
        'conf_ln_b': nrm((N_CONV_LAYERS, D_MODEL), 0.01),
        'conf_w_pw2': nrm((N_CONV_LAYERS, D_MODEL, D_MODEL), D_MODEL ** -0.5),
        'conf_b_pw2': nrm((N_CONV_LAYERS, D_MODEL), 0.01),
        'peer_wq': nrm((DEPTH, D_MODEL, PEER_HEADS * PEER_QDIM), D_MODEL ** -0.5),
        'peer_keys1': nrm((DEPTH, PEER_HEADS, N_KEYS, PEER_QDIM // 2), (PEER_QDIM // 2) ** -0.5),
        'peer_keys2': nrm((DEPTH, PEER_HEADS, N_KEYS, PEER_QDIM // 2), (PEER_QDIM // 2) ** -0.5),
        'peer_u': nrm((DEPTH, N_EXPERTS, D_MODEL), D_MODEL ** -0.5),
        'peer_v': nrm((DEPTH, N_EXPERTS, D_MODEL), PEER_HEADS ** -0.5),
    }


def reference(x, c, ctx, c_ctx, ada_w, ada_b, norm_g, attn_wqkv, attn_q_gain, attn_k_gain, attn_sink, attn_wo,
              lru_w_in, lru_conv_w, lru_conv_b, lru_w_a, lru_b_a, lru_w_x, lru_b_x, lru_lambda, lru_w_out,
              conf_w_pw1, conf_b_pw1, conf_dw_w, conf_dw_b, conf_ln_g, conf_ln_b, conf_w_pw2, conf_b_pw2,
              peer_wq, peer_keys1, peer_keys2, peer_u, peer_v):
    cos, sin = axial_rope_tables(x.shape[1])
    silu_c = jax.nn.silu(c)
    silu_cc = jax.nn.silu(c_ctx)
    xl, xc = x, ctx
    for layer in range(DEPTH):
        need_ctx = layer < DEPTH - 1
        kind = layer % N_MIXERS
        slot = layer // N_MIXERS
        mod_l = (silu_c @ ada_w[layer] + ada_b[layer])[:, None, :]
        mod_c = silu_cc @ ada_w[layer] + ada_b[layer]
        sh1l, sc1l, g1l, sh2l, sc2l, g2l = jnp.split(mod_l, 6, axis=-1)
        sh1c, sc1c, g1c, sh2c, sc2c, g2c = jnp.split(mod_c, 6, axis=-1)
        hl = modulate(rms_norm(xl, norm_g[layer, 0]), sh1l, sc1l)
        hc = modulate(rms_norm(xc, norm_g[layer, 0]), sh1c, sc1c)
        if kind == 0:
            yc, yl = windowed_gqa_mixer(hc, hl, attn_wqkv[slot], attn_q_gain[slot], attn_k_gain[slot],
                                        attn_sink[slot], attn_wo[slot], cos, sin, need_ctx)
        elif kind == 1:
            yc, yl = rglru_mixer(hc, hl, lru_w_in[slot], lru_conv_w[slot], lru_conv_b[slot], lru_w_a[slot],
                                 lru_b_a[slot], lru_w_x[slot], lru_b_x[slot], lru_lambda[slot], lru_w_out[slot],
                                 need_ctx)
        else:
            yc, yl = conformer_conv_mixer(hc, hl, conf_w_pw1[slot], conf_b_pw1[slot], conf_dw_w[slot],
                                          conf_dw_b[slot], conf_ln_g[slot], conf_ln_b[slot], conf_w_pw2[slot],
                                          conf_b_pw2[slot], need_ctx)
        xl = xl + g1l * yl
        hl2 = modulate(rms_norm(xl, norm_g[layer, 1]), sh2l, sc2l)
        xl = xl + g2l * peer_ffn(hl2, peer_wq[layer], peer_keys1[layer], peer_keys2[layer], peer_u[layer], peer_v[layer])
        if need_ctx:
            xc = xc + g1c * yc
            hc2 = modulate(rms_norm(xc, norm_g[layer, 1]), sh2c, sc2c)
            xc = xc + g2c * peer_ffn(hc2, peer_wq[layer], peer_keys1[layer], peer_keys2[layer], peer_u[layer], peer_v[layer])
    return xl
```

```python
import functools

import jax
import jax.numpy as jnp
from jax import lax
from jax.experimental import pallas as pl
from jax.experimental.pallas import tpu as pltpu

F32 = jnp.float32
BF16 = jnp.bfloat16

EPS = 1e-6
NEG_INF = -1e30
GRID_W = 64
N_HEADS = 16
N_KV_HEADS = 4
HEAD_DIM = 64
KV_GROUP = N_HEADS // N_KV_HEADS
WINDOW = 128
ATTN_BLOCK = 128
ROPE_THETA = 10000.0
LRU_BLOCK = 128
LRU_CONV = 4
LRU_CONV_LEFT = 2
LRU_C = 8.0
CONF_KERNEL = 31
N_KEYS = 128
PEER_HEADS = 8
PEER_TOPK = 16

SUBLANES = 8
LANES = 128

TOKEN_TILE = 256
PEER_TOKEN_TILE = 512
PEER_EXPERT_TILE = 1024
SEQ_CHUNK = 256
MIB = 1024 * 1024


def _cparams(semantics, vmem_mib):
    return pltpu.CompilerParams(dimension_semantics=semantics, vmem_limit_bytes=vmem_mib * MIB)


def _dot(a, b):
    return jnp.dot(a, b, preferred_element_type=F32)


def _dot_nt(a, b):
    return lax.dot_general(a, b, (((1,), (1,)), ((), ())), preferred_element_type=F32)


def _split_bf16(x):
    hi = x.astype(BF16)
    lo = (x - hi.astype(F32)).astype(BF16)
    return hi, lo


def _dot3(x, w):
    xh, xl = _split_bf16(x)
    wh, wl = _split_bf16(w)
    return _dot(xh, wh) + _dot(xh, wl) + _dot(xl, wh)


def _rms_mod(x, g, shift, scale):
    ms = jnp.mean(x * x, axis=-1, keepdims=True)
    return (x * lax.rsqrt(ms + EPS) * g) * (1.0 + scale) + shift


def _mod_row_map(n_batch):
    return lambda b, j: (jnp.where(j == 0, n_batch, b), 0, 0)


def _ada_kernel(cv_ref, w_ref, b_ref, o_ref):
    cv = cv_ref[...]
    s = cv * jax.nn.sigmoid(cv)
    o_ref[0] = jnp.dot(s, w_ref[0], preferred_element_type=F32, precision=lax.Precision.HIGHEST) + b_ref[0]


def _ada_call(cvec, ada_w, ada_b):
    depth, d, d6 = ada_w.shape
    rows = cvec.shape[0]
    bn = 1024
    return pl.pallas_call(
        _ada_kernel,
        grid=(depth, d6 // bn),
        in_specs=[
            pl.BlockSpec((rows, d), lambda l, n: (0, 0)),
            pl.BlockSpec((1, d, bn), lambda l, n: (l, 0, n)),
            pl.BlockSpec((1, 1, bn), lambda l, n: (l, 0, n)),
        ],
        out_specs=pl.BlockSpec((1, rows, bn), lambda l, n: (l, 0, n)),
        out_shape=jax.ShapeDtypeStruct((depth, rows, d6), F32),
        compiler_params=_cparams(("arbitrary", "arbitrary"), 32),
        name="ada_mod",
    )(cvec, ada_w, ada_b.reshape(depth, 1, d6))


def _wk_kernel(wq_ref, k1_ref, k2_ref, o1_ref, o2_ref):
    w = wq_ref[0]
    half = N_KEYS
    hp = lax.Precision.HIGHEST
    dn = (((1,), (1,)), ((), ()))
    o1_ref[0] = lax.dot_general(k1_ref[0, 0], w[:, :half], dn, preferred_element_type=F32, precision=hp).astype(BF16)
    o2_ref[0] = lax.dot_general(k2_ref[0, 0], w[:, half:], dn, preferred_element_type=F32, precision=hp).astype(BF16)


def _wk_call(peer_wq, keys1, keys2):
    depth, d, _ = peer_wq.shape
    qd = 2 * N_KEYS
    out = jax.ShapeDtypeStruct((depth, PEER_HEADS * N_KEYS, d), BF16)
    return pl.pallas_call(
        _wk_kernel,
        grid=(depth, PEER_HEADS),
        in_specs=[
            pl.BlockSpec((1, d, qd), lambda l, h: (l, 0, h)),
            pl.BlockSpec((1, 1, N_KEYS, N_KEYS), lambda l, h: (l, h, 0, 0)),
            pl.BlockSpec((1, 1, N_KEYS, N_KEYS), lambda l, h: (l, h, 0, 0)),
        ],
        out_specs=[
            pl.BlockSpec((1, N_KEYS, d), lambda l, h: (l, h, 0)),
            pl.BlockSpec((1, N_KEYS, d), lambda l, h: (l, h, 0)),
        ],
        out_shape=[out, out],
        compiler_params=_cparams(("arbitrary", "arbitrary"), 32),
        name="peer_fold_keys",
    )(peer_wq, keys1, keys2)


def _attn_pre_kernel(x_ref, mod_ref, g_ref, w_ref, bd_ref, qg_ref, kg_ref, cos_ref, sm_ref, sp_ref,
                     q_ref, k_ref, v_ref):
    mod = mod_ref[0]
    h = _rms_mod(x_ref[0], g_ref[...], mod[0:1], mod[1:2]).astype(BF16)
    qkv = _dot(h, w_ref[...])
    nq = N_HEADS * HEAD_DIM
    nk = N_KV_HEADS * HEAD_DIM
    q, k, v = qkv[:, :nq], qkv[:, nq:nq + nk], qkv[:, nq + nk:]

    def head_norm(t, bd, gain):
        hi, lo = _split_bf16(t * t)
        ms = _dot(hi, bd) + _dot(lo, bd)
        return t * lax.rsqrt(ms + EPS) * gain

    def rope(t, width):
        reps = width // LANES
        cs = jnp.tile(cos_ref[...], (1, reps))
        sm = jnp.tile(sm_ref[...], (1, reps))
        sp = jnp.tile(sp_ref[...], (1, reps))
        quarter = HEAD_DIM // 4
        return t * cs + pltpu.roll(t, width - quarter, 1) * sm + pltpu.roll(t, quarter, 1) * sp

    qn = rope(head_norm(q, bd_ref[...], qg_ref[...]), nq)
    kn = rope(head_norm(k, bd_ref[0:nk, 0:nk], kg_ref[...]), nk)
    q_ref[0] = (qn * (HEAD_DIM ** -0.5)).astype(BF16)
    k_ref[0] = kn.astype(BF16)
    v_ref[0] = v.astype(BF16)


def _attn_kernel(n_latent, sink_ref, q_ref, kc_ref, vc_ref, kp_ref, kcur_ref, kn_ref, vp_ref, vcur_ref, vn_ref,
                 o_ref):
    j = pl.program_id(1)
    n_ctx_blocks = kc_ref.shape[1] // ATTN_BLOCK
    bi = j - n_ctx_blocks
    blk = ATTN_BLOCK
    q = q_ref[0]
    kloc = jnp.concatenate([kp_ref[0], kcur_ref[0], kn_ref[0]], axis=0)
    vloc = jnp.concatenate([vp_ref[0], vcur_ref[0], vn_ref[0]], axis=0)
    kc = kc_ref[0]
    vc = vc_ref[0]
    qpos = bi * blk + lax.broadcasted_iota(jnp.int32, (blk, 3 * blk), 0)
    kpos = (bi - 1) * blk + lax.broadcasted_iota(jnp.int32, (blk, 3 * blk), 1)
    valid = (jnp.abs(qpos - kpos) <= WINDOW) & (kpos >= 0) & (kpos < n_latent) & (bi >= 0)
    for h in range(N_HEADS):
        g = h // KV_GROUP
        qh = q[:, h * HEAD_DIM:(h + 1) * HEAD_DIM]
        gs = slice(g * HEAD_DIM, (g + 1) * HEAD_DIM)
        s_loc = jnp.where(valid, _dot_nt(qh, kloc[:, gs]), NEG_INF)
        s_ctx = _dot_nt(qh, kc[:, gs])
        sink = sink_ref[h]
        m = jnp.maximum(jnp.maximum(jnp.max(s_loc, axis=-1, keepdims=True),
                                    jnp.max(s_ctx, axis=-1, keepdims=True)), sink)
        p_loc = jnp.exp(s_loc - m)
        p_ctx = jnp.exp(s_ctx - m)
        den = (jnp.sum(p_loc, axis=-1, keepdims=True) + jnp.sum(p_ctx, axis=-1, keepdims=True)
               + jnp.exp(sink - m))
        o = _dot(p_loc.astype(BF16), vloc[:, gs]) + _dot(p_ctx.astype(BF16), vc[:, gs])
        o_ref[0, :, h * HEAD_DIM:(h + 1) * HEAD_DIM] = (o / den).astype(BF16)


def _rope_tables(n_ctx, n_latent):
    rows = n_latent // GRID_W
    row = jnp.repeat(jnp.arange(rows), GRID_W).astype(F32)
    col = jnp.tile(jnp.arange(GRID_W), rows).astype(F32)
    n_freq = HEAD_DIM // 4
    freqs = ROPE_THETA ** (-jnp.arange(n_freq, dtype=F32) / n_freq)
    ang = jnp.stack([row[:, None] * freqs, col[:, None] * freqs], axis=1)
    cos, sin = jnp.cos(ang), jnp.sin(ang)
    zero = jnp.zeros_like(sin)
    cs = jnp.stack([cos, cos], axis=2).reshape(n_latent, HEAD_DIM)
    sm = jnp.stack([-sin, zero], axis=2).reshape(n_latent, HEAD_DIM)
    sp = jnp.stack([zero, sin], axis=2).reshape(n_latent, HEAD_DIM)

    def full(tab, ctx_val):
        tab = jnp.concatenate([jnp.full((n_ctx, HEAD_DIM), ctx_val, F32), tab], axis=0)
        return jnp.tile(tab, (1, LANES // HEAD_DIM))

    return full(cs, 1.0), full(sm, 0.0), full(sp, 0.0)


def _attn_layer(x, mod, norm_g, w_qkv, q_gain, k_gain, sink, rope_tabs, n_ctx):
    n_batch, s, d = x.shape
    nt = s // TOKEN_TILE
    nq = N_HEADS * HEAD_DIM
    nk = N_KV_HEADS * HEAD_DIM
    eye = jnp.kron(jnp.eye(N_HEADS, dtype=F32), jnp.full((HEAD_DIM, HEAD_DIM), 1.0 / HEAD_DIM, F32)).astype(BF16)
    cs, sm, sp = rope_tabs
    tile_spec = lambda width: pl.BlockSpec((1, TOKEN_TILE, width), lambda b, j: (b, j, 0))
    const = lambda shape: pl.BlockSpec(shape, lambda b, j: (0,) * len(shape))
    tab_spec = pl.BlockSpec((TOKEN_TILE, LANES), lambda b, j: (j, 0))
    q, k, v = pl.pallas_call(
        _attn_pre_kernel,
        grid=(n_batch, nt),
        in_specs=[
            tile_spec(d),
            pl.BlockSpec((1, 6, d), _mod_row_map(n_batch)),
            const((1, d)),
            const((d, nq + 2 * nk)),
            const((nq, nq)),
            const((1, nq)),
            const((1, nk)),
            tab_spec, tab_spec, tab_spec,
        ],
        out_specs=[tile_spec(nq), tile_spec(nk), tile_spec(nk)],
        out_shape=[jax.ShapeDtypeStruct((n_batch, s, nq), BF16),
                   jax.ShapeDtypeStruct((n_batch, s, nk), BF16),
                   jax.ShapeDtypeStruct((n_batch, s, nk), BF16)],
        compiler_params=_cparams(("arbitrary", "arbitrary"), 48),
        name="attn_qkv",
    )(x, mod, norm_g.reshape(1, d), w_qkv.astype(BF16), eye,
      jnp.tile(q_gain, N_HEADS).reshape(1, nq), jnp.tile(k_gain, N_KV_HEADS).reshape(1, nk), cs, sm, sp)

    nb = s // ATTN_BLOCK
    blk_spec = lambda shift: pl.BlockSpec(
        (1, ATTN_BLOCK, nk), lambda b, j: (b, jnp.clip(j + shift, 0, nb - 1), 0))
    ctx_spec = pl.BlockSpec((1, n_ctx, nk), lambda b, j: (b, 0, 0))
    o = pl.pallas_call(
        functools.partial(_attn_kernel, s - n_ctx),
        grid=(n_batch, nb),
        in_specs=[
            pl.BlockSpec(memory_space=pltpu.SMEM),
            pl.BlockSpec((1, ATTN_BLOCK, nq), lambda b, j: (b, j, 0)),
            ctx_spec, ctx_spec,
            blk_spec(-1), blk_spec(0), blk_spec(1),
            blk_spec(-1), blk_spec(0), blk_spec(1),
        ],
        out_specs=pl.BlockSpec((1, ATTN_BLOCK, nq), lambda b, j: (b, j, 0)),
        out_shape=jax.ShapeDtypeStruct((n_batch, s, nq), BF16),
        compiler_params=_cparams(("arbitrary", "arbitrary"), 32),
        name="attn_core",
    )(sink, q, k, v, k, k, k, v, v, v)
    return o


def _lru_pre_kernel(x_ref, mod_ref, g_ref, w_ref, gate_ref, u_ref):
    mod = mod_ref[0]
    h = _rms_mod(x_ref[0], g_ref[...], mod[0:1], mod[1:2]).astype(BF16)
    y = _dot(h, w_ref[...])
    d = gate_ref.shape[-1]
    gate_ref[0] = jax.nn.gelu(y[:, :d]).astype(BF16)
    u_ref[0] = y[:, d:]


def _padded_copy(src_ref, pad_scr, n_ctx, pad):
    s = src_ref.shape[1]
    c = pad_scr.shape[1]
    zeros = jnp.zeros((pad, c), F32)
    pad_scr[0:pad] = zeros
    pad_scr[pad:pad + n_ctx] = src_ref[0, 0:n_ctx]
    pad_scr[pad + n_ctx:2 * pad + n_ctx] = zeros
    pad_scr[2 * pad + n_ctx:2 * pad + s] = src_ref[0, n_ctx:s]
    pad_scr[2 * pad + s:3 * pad + s] = zeros


def _padded_row(r, n_ctx, pad):
    return r + pad if r < n_ctx else r + 2 * pad


def _lru_seq_kernel(n_ctx, u_ref, gate_ref, cw_ref, cb_ref, wa_ref, ba_ref, wx_ref, bx_ref, lam_ref, y_ref,
                    pad_scr, a0_scr, b0_scr, a1_scr, b1_scr):
    s = u_ref.shape[1]
    pad = SUBLANES
    _padded_copy(u_ref, pad_scr, n_ctx, pad)
    a_scr = (a0_scr, a1_scr)
    b_scr = (b0_scr, b1_scr)
    cw = cw_ref[...]
    for ci in range(s // SEQ_CHUNK):
        r0 = ci * SEQ_CHUNK
        base = _padded_row(r0, n_ctx, pad) - LRU_CONV_LEFT
        u = cb_ref[...] + sum(pad_scr[base + k:base + k + SEQ_CHUNK] * cw[k:k + 1] for k in range(LRU_CONV))
        for d in range(2):
            r = jax.nn.sigmoid(_dot3(u, wa_ref[d, 0]) + ba_ref[d:d + 1])
            i = jax.nn.sigmoid(_dot3(u, wx_ref[d, 0]) + bx_ref[d:d + 1])
            log_a = -LRU_C * r * jax.nn.softplus(-lam_ref[d:d + 1])
            a = jnp.exp(log_a)
            a_scr[d][r0:r0 + SEQ_CHUNK] = a
            b_scr[d][r0:r0 + SEQ_CHUNK] = jnp.sqrt(jnp.tanh(-log_a) * (1.0 + a * a)) * (i * u)

    c = u_ref.shape[2]
    rid = lax.broadcasted_iota(jnp.int32, (SUBLANES, c), 0)
    n_steps = s // SUBLANES
    n_ctx_steps = n_ctx // SUBLANES

    def scan8(a, b, h_prev, reverse):
        for sh in (1, 2, 4):
            if reverse:
                keep = rid < SUBLANES - sh
                amt = SUBLANES - sh
            else:
                keep = rid >= sh
                amt = sh
            a_sh = jnp.where(keep, pltpu.roll(a, amt, 0), 1.0)
            b_sh = jnp.where(keep, pltpu.roll(b, amt, 0), 0.0)
            b = a * b_sh + b
            a = a * a_sh
        return b + a * h_prev

    def step(n, carry):
        hf, hb = carry
        rf = pl.multiple_of(n * SUBLANES, SUBLANES)
        nb = jnp.where(n < n_ctx_steps, n_ctx_steps - 1 - n, n_steps - 1 - n + n_ctx_steps)
        rb = pl.multiple_of(nb * SUBLANES, SUBLANES)
        out_f = scan8(a0_scr[pl.ds(rf, SUBLANES)], b0_scr[pl.ds(rf, SUBLANES)], hf, False)
        out_b = scan8(a1_scr[pl.ds(rb, SUBLANES)], b1_scr[pl.ds(rb, SUBLANES)], hb, True)
        b0_scr[pl.ds(rf, SUBLANES)] = out_f
        b1_scr[pl.ds(rb, SUBLANES)] = out_b
        return out_f[SUBLANES - 1:SUBLANES], out_b[0:1]

    zero = jnp.zeros((1, c), F32)
    lax.fori_loop(0, n_steps, step, (zero, zero))
    for ci in range(s // SEQ_CHUNK):
        rows = slice(ci * SEQ_CHUNK, (ci + 1) * SEQ_CHUNK)
        y_ref[0, rows] = ((b0_scr[rows] + b1_scr[rows]) * gate_ref[0, rows].astype(F32)).astype(BF16)


def _lru_layer(x, mod, norm_g, w_in, conv_w, conv_b, w_a, b_a, w_x, b_x, lam, n_ctx):
    n_batch, s, d = x.shape
    nt = s // TOKEN_TILE
    tile_spec = pl.BlockSpec((1, TOKEN_TILE, d), lambda b, j: (b, j, 0))
    gate, u = pl.pallas_call(
        _lru_pre_kernel,
        grid=(n_batch, nt),
        in_specs=[
            tile_spec,
            pl.BlockSpec((1, 6, d), _mod_row_map(n_batch)),
            pl.BlockSpec((1, d), lambda b, j: (0, 0)),
            pl.BlockSpec((d, 2 * d), lambda b, j: (0, 0)),
        ],
        out_specs=[tile_spec, tile_spec],
        out_shape=[jax.ShapeDtypeStruct((n_batch, s, d), BF16), jax.ShapeDtypeStruct((n_batch, s, d), F32)],
        compiler_params=_cparams(("arbitrary", "arbitrary"), 48),
        name="lru_in",
    )(x, mod, norm_g.reshape(1, d), w_in.astype(BF16))

    cblk = LRU_BLOCK
    ncb = d // cblk
    seq_spec = pl.BlockSpec((1, s, cblk), lambda b, c: (b, 0, c))
    vec2 = pl.BlockSpec((2, cblk), lambda b, c: (0, c))
    wspec = pl.BlockSpec((2, 1, cblk, cblk), lambda b, c: (0, c, 0, 0))
    seq_scr = pltpu.VMEM((s, cblk), F32)
    return pl.pallas_call(
        functools.partial(_lru_seq_kernel, n_ctx),
        grid=(n_batch, ncb),
        in_specs=[
            seq_spec, seq_spec,
            pl.BlockSpec((LRU_CONV, cblk), lambda b, c: (0, c)),
            pl.BlockSpec((1, cblk), lambda b, c: (0, c)),
            wspec, vec2, wspec, vec2, vec2,
        ],
        out_specs=seq_spec,
        out_shape=jax.ShapeDtypeStruct((n_batch, s, d), BF16),
        scratch_shapes=[pltpu.VMEM((s + 3 * SUBLANES, cblk), F32), seq_scr, seq_scr, seq_scr, seq_scr],
        compiler_params=_cparams(("arbitrary", "arbitrary"), 48),
        name="lru_scan",
    )(u, gate, conv_w, conv_b.reshape(1, d), w_a, b_a, w_x, b_x, lam)


def _conf_pre_kernel(x_ref, mod_ref, g_ref, w_ref, b_ref, o_ref):
    mod = mod_ref[0]
    h = _rms_mod(x_ref[0], g_ref[...], mod[0:1], mod[1:2]).astype(BF16)
    y = _dot(h, w_ref[...]) + b_ref[...]
    d = o_ref.shape[-1]
    o_ref[0] = y[:, :d] * jax.nn.sigmoid(y[:, d:])


def _conf_conv_kernel(n_ctx, u_ref, w_ref, b_ref, o_ref, pad_scr):
    s = u_ref.shape[1]
    pad = 2 * SUBLANES
    left = CONF_KERNEL // 2
    _padded_copy(u_ref, pad_scr, n_ctx, pad)
    w = w_ref[...]
    for ci in range(s // SEQ_CHUNK):
        r0 = ci * SEQ_CHUNK
        base = _padded_row(r0, n_ctx, pad) - left
        acc = b_ref[...] + pad_scr[base:base + SEQ_CHUNK] * w[0:1]
        for k in range(1, CONF_KERNEL):
            acc = acc + pad_scr[base + k:base + k + SEQ_CHUNK] * w[k:k + 1]
        o_ref[0, r0:r0 + SEQ_CHUNK] = acc


def _conf_layer(x, mod, norm_g, w_pw1, b_pw1, dw_w, dw_b, n_ctx):
    n_batch, s, d = x.shape
    nt = s // TOKEN_TILE
    tile_spec = pl.BlockSpec((1, TOKEN_TILE, d), lambda b, j: (b, j, 0))
    glu = pl.pallas_call(
        _conf_pre_kernel,
        grid=(n_batch, nt),
        in_specs=[
            tile_spec,
            pl.BlockSpec((1, 6, d), _mod_row_map(n_batch)),
            pl.BlockSpec((1, d), lambda b, j: (0, 0)),
            pl.BlockSpec((d, 2 * d), lambda b, j: (0, 0)),
            pl.BlockSpec((1, 2 * d), lambda b, j: (0, 0)),
        ],
        out_specs=tile_spec,
        out_shape=jax.ShapeDtypeStruct((n_batch, s, d), F32),
        compiler_params=_cparams(("arbitrary", "arbitrary"), 48),
        name="conf_in",
    )(x, mod, norm_g.reshape(1, d), w_pw1.astype(BF16), b_pw1.reshape(1, 2 * d))

    cblk = LANES
    seq_spec = pl.BlockSpec((1, s, cblk), lambda b, c: (b, 0, c))
    return pl.pallas_call(
        functools.partial(_conf_conv_kernel, n_ctx),
        grid=(n_batch, d // cblk),
        in_specs=[
            seq_spec,
            pl.BlockSpec((CONF_KERNEL, cblk), lambda b, c: (0, c)),
            pl.BlockSpec((1, cblk), lambda b, c: (0, c)),
        ],
        out_specs=seq_spec,
        out_shape=jax.ShapeDtypeStruct((n_batch, s, d), F32),
        scratch_shapes=[pltpu.VMEM((s + 6 * SUBLANES, cblk), F32)],
        compiler_params=_cparams(("arbitrary", "arbitrary"), 32),
        name="conf_dwconv",
    )(glu, dw_w, dw_b.reshape(1, d))


def _compare_exchange(planes, i, j):
    hi = jnp.maximum(planes[i], planes[j])
    lo = jnp.minimum(planes[i], planes[j])
    planes[i], planes[j] = hi, lo


def _batcher_pairs(n):
    pairs = []

    def merge(lo, m, r):
        step = 2 * r
        if step < m:
            merge(lo, m, step)
            merge(lo + r, m, step)
            for i in range(lo + r, lo + m - r, step):
                pairs.append((i, i + r))
        else:
            pairs.append((lo, lo + r))

    def sort(lo, m):
        if m > 1:
            half = m // 2
            sort(lo, half)
            sort(lo + half, half)
            merge(lo, m, 1)

    sort(0, n)
    return pairs


_SORT16 = _batcher_pairs(PEER_TOPK)


def _sort_desc(planes):
    planes = list(planes)
    for i, j in _SORT16:
        _compare_exchange(planes, i, j)
    return planes


def _merge_top(a, b):
    n = len(a)
    planes = [jnp.maximum(a[i], b[n - 1 - i]) for i in range(n)]
    d = n // 2
    while d >= 1:
        for i in range(n):
            if i & d == 0:
                _compare_exchange(planes, i, i + d)
        d //= 2
    return planes


def _top_sorted(groups):
    groups = [_sort_desc(g) for g in groups]
    while len(groups) > 1:
        groups = [_merge_top(groups[i], groups[i + 1]) for i in range(0, len(groups), 2)]
    return groups[0]


def _peer_stats(s1_scr, s2_scr, lanes, r2_ref, e2_ref, beta_ref, c1_ref):
    k = PEER_TOPK

    def top_planes(scr):
        groups = [[scr[pl.ds(g * k + i, PEER_HEADS, stride=N_KEYS), :] for i in range(k)]
                  for g in range(N_KEYS // k)]
        return _top_sorted(groups)

    v1 = top_planes(s1_scr)
    v2 = top_planes(s2_scr)
    cands = [v1[a] + v2[b] for a in range(k) for b in range(k) if (a + 1) * (b + 1) <= k]
    fill = jnp.full_like(v1[0], -jnp.inf)
    cands = cands + [fill] * (-len(cands) % k)
    top = _top_sorted([cands[i:i + k] for i in range(0, len(cands), k)])
    tau = top[k - 1]
    z = sum(jnp.exp(t - top[0]) for t in top)
    inv_z = 1.0 / z
    for h in range(PEER_HEADS):
        hs = slice(h, h + 1)
        s1 = s1_scr[h * N_KEYS:(h + 1) * N_KEYS, :]
        s2 = s2_scr[h * N_KEYS:(h + 1) * N_KEYS, :]
        tau_h = tau[hs]
        beta = jnp.zeros_like(s1)
        rank2 = jnp.zeros_like(s2)
        for b in range(k):
            v2b = v2[b][hs]
            beta = beta + jnp.where(s1 + v2b >= tau_h, 1.0, 0.0)
            rank2 = rank2 + jnp.where(v2b > s2, 1.0, 0.0)
        beta_ref[h, :, lanes] = beta
        r2_ref[h, :, lanes] = rank2
        e2_ref[h, :, lanes] = jnp.exp(s2 - v2[0][hs])
        c1_ref[h, :, lanes] = jnp.exp(s1 - v1[0][hs]) * inv_z[hs]


def _post_kernel(ln_silu, yin_ref, x_ref, mod_ref, w_ref, b_ref, ng_ref, lng_ref, lnb_ref, wk1_ref, wk2_ref,
                 x1_ref, h2t_ref, r2_ref, e2_ref, beta_ref, c1_ref, s1_scr, s2_scr):
    mod = mod_ref[0]
    yin = yin_ref[0]
    if ln_silu:
        mu = jnp.mean(yin, axis=-1, keepdims=True)
        cen = yin - mu
        var = jnp.mean(cen * cen, axis=-1, keepdims=True)
        t = cen * lax.rsqrt(var + EPS) * lng_ref[...] + lnb_ref[...]
        yin = t * jax.nn.sigmoid(t)
    y = _dot(yin.astype(BF16), w_ref[...]) + b_ref[...]
    x1 = x_ref[0] + mod[2:3] * y
    x1_ref[0] = x1
    h2 = _rms_mod(x1, ng_ref[...], mod[3:4], mod[4:5])
    h2t = h2.T.astype(BF16)
    h2t_ref[...] = h2t
    s1 = _dot(wk1_ref[...], h2t)
    s2 = _dot(wk2_ref[...], h2t)
    for lc in range(s1_scr.shape[0]):
        lanes = slice(lc * LANES, (lc + 1) * LANES)
        s1_scr[lc] = s1[:, lanes]
        s2_scr[lc] = s2[:, lanes]
        _peer_stats(s1_scr.at[lc], s2_scr.at[lc], lanes, r2_ref, e2_ref, beta_ref, c1_ref)


def _post_call(ln_silu, yin, x, mod, w, bias, norm_g2, ln_g, ln_b, wk1, wk2):
    n_batch, s, d = x.shape
    nt = s // TOKEN_TILE
    n_tok = n_batch * s
    tile_spec = pl.BlockSpec((1, TOKEN_TILE, d), lambda b, j: (b, j, 0))
    const = lambda shape: pl.BlockSpec(shape, lambda b, j: (0,) * len(shape))
    side_spec = pl.BlockSpec((PEER_HEADS, N_KEYS, TOKEN_TILE), lambda b, j: (0, 0, b * nt + j))
    side_shape = jax.ShapeDtypeStruct((PEER_HEADS, N_KEYS, n_tok), F32)
    hk = PEER_HEADS * N_KEYS
    return pl.pallas_call(
        functools.partial(_post_kernel, ln_silu),
        grid=(n_batch, nt),
        in_specs=[
            tile_spec, tile_spec,
            pl.BlockSpec((1, 6, d), _mod_row_map(n_batch)),
            const((yin.shape[-1], d)), const((1, d)), const((1, d)), const((1, d)), const((1, d)),
            const((hk, d)), const((hk, d)),
        ],
        out_specs=[
            tile_spec,
            pl.BlockSpec((d, TOKEN_TILE), lambda b, j: (0, b * nt + j)),
            side_spec, side_spec, side_spec, side_spec,
        ],
        out_shape=[
            jax.ShapeDtypeStruct((n_batch, s, d), F32),
            jax.ShapeDtypeStruct((d, n_tok), BF16),
            side_shape, side_shape, side_shape, side_shape,
        ],
        scratch_shapes=[pltpu.VMEM((TOKEN_TILE // LANES, hk, LANES), F32)] * 2,
        compiler_params=_cparams(("arbitrary", "arbitrary"), 56),
        name="mixer_out_peer_stats",
    )(yin, x, mod, w.astype(BF16), bias.reshape(1, d), norm_g2.reshape(1, d), ln_g.reshape(1, d),
      ln_b.reshape(1, d), wk1, wk2)


def _peer_kernel(h2t_ref, r2_ref, e2_ref, beta_ref, c1_ref, u_ref, vt_ref, x1_ref, moda_ref, modb_ref, o_ref,
                 act_scr, g_scr, acc_scr):
    e = pl.program_id(1)
    n_e = pl.num_programs(1)

    @pl.when(e == 0)
    def _():
        acc_scr[...] = jnp.zeros_like(acc_scr)

    act_scr[...] = _dot(u_ref[...], h2t_ref[...])
    chunks = act_scr.shape[0] // N_KEYS

    def chunk(c, carry):
        i1 = e * chunks + c
        rows = pl.ds(pl.multiple_of(c * N_KEYS, N_KEYS), N_KEYS)
        gate = jnp.zeros((N_KEYS, act_scr.shape[1]), F32)
        for h in range(PEER_HEADS):
            beta = beta_ref[h, pl.ds(i1, 1), :]
            c1 = c1_ref[h, pl.ds(i1, 1), :]
            gate = gate + jnp.where(r2_ref[h] < beta, e2_ref[h] * c1, 0.0)
        g_scr[rows, :] = (gate * jax.nn.gelu(act_scr[rows, :])).astype(BF16)
        return carry

    lax.fori_loop(0, chunks, chunk, 0)
    acc_scr[...] += _dot(vt_ref[...], g_scr[...])

    @pl.when(e == n_e - 1)
    def _():
        out = acc_scr[...].T
        half = out.shape[0] // 2
        o_ref[0:half] = x1_ref[0:half] + moda_ref[0, 5:6] * out[0:half]
        o_ref[half:] = x1_ref[half:] + modb_ref[0, 5:6] * out[half:]


def _peer_call(h2t, r2, e2, beta, c1, u_bf, vt_bf, x1, mod, n_batch):
    d, n_tok = h2t.shape
    n_exp = u_bf.shape[0]
    tt, et = PEER_TOKEN_TILE, PEER_EXPERT_TILE
    nt = (n_tok // n_batch) // TOKEN_TILE
    side_spec = pl.BlockSpec((PEER_HEADS, N_KEYS, tt), lambda i, e: (0, 0, i))

    def mod_map(half):
        def index(i, e):
            t = i * (tt // TOKEN_TILE) + half
            return (jnp.where(t % nt == 0, n_batch, t // nt), 0, 0)
        return index

    return pl.pallas_call(
        _peer_kernel,
        grid=(n_tok // tt, n_exp // et),
        in_specs=[
            pl.BlockSpec((d, tt), lambda i, e: (0, i)),
            side_spec, side_spec, side_spec, side_spec,
            pl.BlockSpec((et, d), lambda i, e: (e, 0)),
            pl.BlockSpec((d, et), lambda i, e: (0, e)),
            pl.BlockSpec((tt, d), lambda i, e: (i, 0)),
            pl.BlockSpec((1, 6, d), mod_map(0)),
            pl.BlockSpec((1, 6, d), mod_map(1)),
        ],
        out_specs=pl.BlockSpec((tt, d), lambda i, e: (i, 0)),
        out_shape=jax.ShapeDtypeStruct((n_tok, d), F32),
        scratch_shapes=[pltpu.VMEM((et, tt), F32), pltpu.VMEM((et, tt), BF16), pltpu.VMEM((d, tt), F32)],
        compiler_params=_cparams(("arbitrary", "arbitrary"), 56),
        name="peer_dense",
    )(h2t, r2, e2, beta, c1, u_bf, vt_bf, x1.reshape(n_tok, d), mod, mod)


def kernel(x, c, ctx, c_ctx, ada_w, ada_b, norm_g, attn_wqkv, attn_q_gain, attn_k_gain, attn_sink, attn_wo,
           lru_w_in, lru_conv_w, lru_conv_b, lru_w_a, lru_b_a, lru_w_x, lru_b_x, lru_lambda, lru_w_out,
           conf_w_pw1, conf_b_pw1, conf_dw_w, conf_dw_b, conf_ln_g, conf_ln_b, conf_w_pw2, conf_b_pw2,
           peer_wq, peer_keys1, peer_keys2, peer_u, peer_v):
    n_batch, n_latent, d = x.shape
    n_ctx = ctx.shape[1]
    depth = ada_w.shape[0]
    assert n_ctx == TOKEN_TILE and n_latent % TOKEN_TILE == 0
    assert (n_batch * (n_ctx + n_latent)) % PEER_TOKEN_TILE == 0

    xs = jnp.concatenate([ctx, x], axis=1)
    mod_rows = -(-(n_batch + 1) // SUBLANES) * SUBLANES
    cvec = jnp.concatenate([c, c_ctx[None], jnp.zeros((mod_rows - n_batch - 1, d), F32)], axis=0)
    mods = _ada_call(cvec, ada_w, ada_b).reshape(depth, mod_rows, 6, d)
    wk1_all, wk2_all = _wk_call(peer_wq, peer_keys1, peer_keys2)
    rope_tabs = _rope_tables(n_ctx, n_latent)
    zeros_d = jnp.zeros((d,), F32)
    ones_d = jnp.ones((d,), F32)

    for layer in range(depth):
        kind, slot = layer % 3, layer // 3
        mod = mods[layer]
        if kind == 0:
            yin = _attn_layer(xs, mod, norm_g[layer, 0], attn_wqkv[slot], attn_q_gain[slot], attn_k_gain[slot],
                              attn_sink[slot], rope_tabs, n_ctx)
            post = (False, yin, xs, mod, attn_wo[slot], zeros_d, norm_g[layer, 1], ones_d, zeros_d)
        elif kind == 1:
            yin = _lru_layer(xs, mod, norm_g[layer, 0], lru_w_in[slot], lru_conv_w[slot], lru_conv_b[slot],
                             lru_w_a[slot], lru_b_a[slot], lru_w_x[slot], lru_b_x[slot], lru_lambda[slot], n_ctx)
            post = (False, yin, xs, mod, lru_w_out[slot], zeros_d, norm_g[layer, 1], ones_d, zeros_d)
        else:
            yin = _conf_layer(xs, mod, norm_g[layer, 0], conf_w_pw1[slot], conf_b_pw1[slot], conf_dw_w[slot],
                              conf_dw_b[slot], n_ctx)
            post = (True, yin, xs, mod, conf_w_pw2[slot], conf_b_pw2[slot], norm_g[layer, 1], conf_ln_g[slot],
                    conf_ln_b[slot])
        x1, h2t, r2, e2, beta, c1 = _post_call(*post, wk1_all[layer], wk2_all[layer])
        x2 = _peer_call(h2t, r2, e2, beta, c1, peer_u[layer].astype(BF16), peer_v[layer].T.astype(BF16), x1, mod,
                        n_batch)
        xs = x2.reshape(n_batch, n_ctx + n_latent, d)
    return xs[:, n_ctx:, :]
```

```python
import functools

import jax
import jax.numpy as jnp
from jax import lax
from jax.experimental import pallas as pl
from jax.experimental.pallas import tpu as pltpu

F32 = jnp.float32
BF16 = jnp.bfloat16

EPS = 1e-6
NEG_INF = -1e30
GRID_W = 64
N_HEADS = 16
N_KV_HEADS = 4
HEAD_DIM = 64
KV_GROUP = N_HEADS // N_KV_HEADS
WINDOW = 128
ATTN_BLOCK = 128
ROPE_THETA = 10000.0
LRU_BLOCK = 128
LRU_CONV = 4
LRU_CONV_LEFT = 2
LRU_C = 8.0
CONF_KERNEL = 31
N_KEYS = 128
PEER_HEADS = 8
PEER_TOPK = 16
GELU_C0 = 0.7978845608028654
GELU_C1 = GELU_C0 * 0.044715

SUBLANES = 8
LANES = 128

TOKEN_TILE = 256
PEER_TOKEN_TILE = 512
PEER_EXPERT_TILE = 512
PEER_LANE_SPLIT = 256
SEQ_CHUNK = 256
MIB = 1024 * 1024


def _cparams(semantics, vmem_mib):
    return pltpu.CompilerParams(dimension_semantics=semantics, vmem_limit_bytes=vmem_mib * MIB)


def _pack_rows(x):
    return pltpu.bitcast(x.astype(BF16), jnp.uint32)


def _unpack_rows(w):
    return pltpu.bitcast(w, BF16)


def _dup_bf16_words(x):
    bits = pltpu.bitcast(x.astype(BF16).astype(F32), jnp.uint32)
    return bits | (bits >> 16)


def _dot(a, b):
    return jnp.dot(a, b, preferred_element_type=F32)


def _dot_nt(a, b):
    return lax.dot_general(a, b, (((1,), (1,)), ((), ())), preferred_element_type=F32)


def _split_bf16(x):
    hi = x.astype(BF16)
    lo = (x - hi.astype(F32)).astype(BF16)
    return hi, lo


def _dot3(x, w):
    xh, xl = _split_bf16(x)
    wh, wl = _split_bf16(w)
    return _dot(xh, wh) + _dot(xh, wl) + _dot(xl, wh)


def _rms_mod(x, g, shift, scale):
    ms = jnp.mean(x * x, axis=-1, keepdims=True)
    return (x * lax.rsqrt(ms + EPS) * g) * (1.0 + scale) + shift


def _mod_row_map(n_batch):
    return lambda b, j: (jnp.where(j == 0, n_batch, b), 0, 0)


def _ada_kernel(cv_ref, w_ref, b_ref, o_ref):
    cv = cv_ref[...]
    s = cv * jax.nn.sigmoid(cv)
    o_ref[0] = jnp.dot(s, w_ref[0], preferred_element_type=F32, precision=lax.Precision.HIGHEST) + b_ref[0]


def _ada_call(cvec, ada_w, ada_b):
    depth, d, d6 = ada_w.shape
    rows = cvec.shape[0]
    bn = 1024
    return pl.pallas_call(
        _ada_kernel,
        grid=(depth, d6 // bn),
        in_specs=[
            pl.BlockSpec((rows, d), lambda l, n: (0, 0)),
            pl.BlockSpec((1, d, bn), lambda l, n: (l, 0, n)),
            pl.BlockSpec((1, 1, bn), lambda l, n: (l, 0, n)),
        ],
        out_specs=pl.BlockSpec((1, rows, bn), lambda l, n: (l, 0, n)),
        out_shape=jax.ShapeDtypeStruct((depth, rows, d6), F32),
        compiler_params=_cparams(("arbitrary", "arbitrary"), 32),
        name="ada_mod",
    )(cvec, ada_w, ada_b.reshape(depth, 1, d6))


def _wk_kernel(wq_ref, k1_ref, k2_ref, o1_ref, o2_ref):
    w = wq_ref[0]
    half = N_KEYS
    hp = lax.Precision.HIGHEST
    dn = (((1,), (1,)), ((), ()))
    o1_ref[0] = lax.dot_general(k1_ref[0, 0], w[:, :half], dn, preferred_element_type=F32, precision=hp).astype(BF16)
    o2_ref[0] = lax.dot_general(k2_ref[0, 0], w[:, half:], dn, preferred_element_type=F32, precision=hp).astype(BF16)


def _wk_call(peer_wq, keys1, keys2):
    depth, d, _ = peer_wq.shape
    qd = 2 * N_KEYS
    out = jax.ShapeDtypeStruct((depth, PEER_HEADS * N_KEYS, d), BF16)
    return pl.pallas_call(
        _wk_kernel,
        grid=(depth, PEER_HEADS),
        in_specs=[
            pl.BlockSpec((1, d, qd), lambda l, h: (l, 0, h)),
            pl.BlockSpec((1, 1, N_KEYS, N_KEYS), lambda l, h: (l, h, 0, 0)),
            pl.BlockSpec((1, 1, N_KEYS, N_KEYS), lambda l, h: (l, h, 0, 0)),
        ],
        out_specs=[
            pl.BlockSpec((1, N_KEYS, d), lambda l, h: (l, h, 0)),
            pl.BlockSpec((1, N_KEYS, d), lambda l, h: (l, h, 0)),
        ],
        out_shape=[out, out],
        compiler_params=_cparams(("arbitrary", "arbitrary"), 32),
        name="peer_fold_keys",
    )(peer_wq, keys1, keys2)


def _attn_pre_kernel(x_ref, mod_ref, g_ref, w_ref, bd_ref, qg_ref, kg_ref, cos_ref, sm_ref, sp_ref,
                     q_ref, k_ref, v_ref):
    mod = mod_ref[0]
    h = _rms_mod(x_ref[0], g_ref[...], mod[0:1], mod[1:2]).astype(BF16)
    qkv = _dot(h, w_ref[...])
    nq = N_HEADS * HEAD_DIM
    nk = N_KV_HEADS * HEAD_DIM
    q, k, v = qkv[:, :nq], qkv[:, nq:nq + nk], qkv[:, nq + nk:]

    def head_norm(t, bd, gain):
        hi, lo = _split_bf16(t * t)
        ms = _dot(hi, bd) + _dot(lo, bd)
        return t * lax.rsqrt(ms + EPS) * gain

    def rope(t, width):
        reps = width // LANES
        cs = jnp.tile(cos_ref[...], (1, reps))
        sm = jnp.tile(sm_ref[...], (1, reps))
        sp = jnp.tile(sp_ref[...], (1, reps))
        quarter = HEAD_DIM // 4
        return t * cs + pltpu.roll(t, width - quarter, 1) * sm + pltpu.roll(t, quarter, 1) * sp

    qn = rope(head_norm(q, bd_ref[...], qg_ref[...]), nq)
    kn = rope(head_norm(k, bd_ref[0:nk, 0:nk], kg_ref[...]), nk)
    q_ref[0] = (qn * (HEAD_DIM ** -0.5)).astype(BF16)
    k_ref[0] = kn.astype(BF16)
    v_ref[0] = v.astype(BF16)


def _attn_kernel(n_latent, sink_ref, q_ref, kc_ref, vc_ref, kp_ref, kcur_ref, kn_ref, vp_ref, vcur_ref, vn_ref,
                 o_ref):
    j = pl.program_id(1)
    n_ctx_blocks = kc_ref.shape[1] // ATTN_BLOCK
    bi = j - n_ctx_blocks
    blk = ATTN_BLOCK
    q = q_ref[0]
    kloc = jnp.concatenate([kp_ref[0], kcur_ref[0], kn_ref[0]], axis=0)
    vloc = jnp.concatenate([vp_ref[0], vcur_ref[0], vn_ref[0]], axis=0)
    kc = kc_ref[0]
    vc = vc_ref[0]
    qpos = bi * blk + lax.broadcasted_iota(jnp.int32, (blk, 3 * blk), 0)
    kpos = (bi - 1) * blk + lax.broadcasted_iota(jnp.int32, (blk, 3 * blk), 1)
    valid = (jnp.abs(qpos - kpos) <= WINDOW) & (kpos >= 0) & (kpos < n_latent) & (bi >= 0)
    for h in range(N_HEADS):
        g = h // KV_GROUP
        qh = q[:, h * HEAD_DIM:(h + 1) * HEAD_DIM]
        gs = slice(g * HEAD_DIM, (g + 1) * HEAD_DIM)
        s_loc = jnp.where(valid, _dot_nt(qh, kloc[:, gs]), NEG_INF)
        s_ctx = _dot_nt(qh, kc[:, gs])
        sink = sink_ref[h]
        m = jnp.maximum(jnp.maximum(jnp.max(s_loc, axis=-1, keepdims=True),
                                    jnp.max(s_ctx, axis=-1, keepdims=True)), sink)
        p_loc = jnp.exp(s_loc - m)
        p_ctx = jnp.exp(s_ctx - m)
        den = (jnp.sum(p_loc, axis=-1, keepdims=True) + jnp.sum(p_ctx, axis=-1, keepdims=True)
               + jnp.exp(sink - m))
        o = _dot(p_loc.astype(BF16), vloc[:, gs]) + _dot(p_ctx.astype(BF16), vc[:, gs])
        o_ref[0, :, h * HEAD_DIM:(h + 1) * HEAD_DIM] = (o / den).astype(BF16)


def _rope_tables(n_ctx, n_latent):
    rows = n_latent // GRID_W
    row = jnp.repeat(jnp.arange(rows), GRID_W).astype(F32)
    col = jnp.tile(jnp.arange(GRID_W), rows).astype(F32)
    n_freq = HEAD_DIM // 4
    freqs = ROPE_THETA ** (-jnp.arange(n_freq, dtype=F32) / n_freq)
    ang = jnp.stack([row[:, None] * freqs, col[:, None] * freqs], axis=1)
    cos, sin = jnp.cos(ang), jnp.sin(ang)
    zero = jnp.zeros_like(sin)
    cs = jnp.stack([cos, cos], axis=2).reshape(n_latent, HEAD_DIM)
    sm = jnp.stack([-sin, zero], axis=2).reshape(n_latent, HEAD_DIM)
    sp = jnp.stack([zero, sin], axis=2).reshape(n_latent, HEAD_DIM)

    def full(tab, ctx_val):
        tab = jnp.concatenate([jnp.full((n_ctx, HEAD_DIM), ctx_val, F32), tab], axis=0)
        return jnp.tile(tab, (1, LANES // HEAD_DIM))

    return full(cs, 1.0), full(sm, 0.0), full(sp, 0.0)


def _attn_layer(x, mod, norm_g, w_qkv, q_gain, k_gain, sink, rope_tabs, n_ctx):
    n_batch, s, d = x.shape
    nt = s // TOKEN_TILE
    nq = N_HEADS * HEAD_DIM
    nk = N_KV_HEADS * HEAD_DIM
    eye = jnp.kron(jnp.eye(N_HEADS, dtype=F32), jnp.full((HEAD_DIM, HEAD_DIM), 1.0 / HEAD_DIM, F32)).astype(BF16)
    cs, sm, sp = rope_tabs
    tile_spec = lambda width: pl.BlockSpec((1, TOKEN_TILE, width), lambda b, j: (b, j, 0))
    const = lambda shape: pl.BlockSpec(shape, lambda b, j: (0,) * len(shape))
    tab_spec = pl.BlockSpec((TOKEN_TILE, LANES), lambda b, j: (j, 0))
    q, k, v = pl.pallas_call(
        _attn_pre_kernel,
        grid=(n_batch, nt),
        in_specs=[
            tile_spec(d),
            pl.BlockSpec((1, 6, d), _mod_row_map(n_batch)),
            const((1, d)),
            const((d, nq + 2 * nk)),
            const((nq, nq)),
            const((1, nq)),
            const((1, nk)),
            tab_spec, tab_spec, tab_spec,
        ],
        out_specs=[tile_spec(nq), tile_spec(nk), tile_spec(nk)],
        out_shape=[jax.ShapeDtypeStruct((n_batch, s, nq), BF16),
                   jax.ShapeDtypeStruct((n_batch, s, nk), BF16),
                   jax.ShapeDtypeStruct((n_batch, s, nk), BF16)],
        compiler_params=_cparams(("arbitrary", "arbitrary"), 48),
        name="attn_qkv",
    )(x, mod, norm_g.reshape(1, d), w_qkv.astype(BF16), eye,
      jnp.tile(q_gain, N_HEADS).reshape(1, nq), jnp.tile(k_gain, N_KV_HEADS).reshape(1, nk), cs, sm, sp)

    nb = s // ATTN_BLOCK
    blk_spec = lambda shift: pl.BlockSpec(
        (1, ATTN_BLOCK, nk), lambda b, j: (b, jnp.clip(j + shift, 0, nb - 1), 0))
    ctx_spec = pl.BlockSpec((1, n_ctx, nk), lambda b, j: (b, 0, 0))
    o = pl.pallas_call(
        functools.partial(_attn_kernel, s - n_ctx),
        grid=(n_batch, nb),
        in_specs=[
            pl.BlockSpec(memory_space=pltpu.SMEM),
            pl.BlockSpec((1, ATTN_BLOCK, nq), lambda b, j: (b, j, 0)),
            ctx_spec, ctx_spec,
            blk_spec(-1), blk_spec(0), blk_spec(1),
            blk_spec(-1), blk_spec(0), blk_spec(1),
        ],
        out_specs=pl.BlockSpec((1, ATTN_BLOCK, nq), lambda b, j: (b, j, 0)),
        out_shape=jax.ShapeDtypeStruct((n_batch, s, nq), BF16),
        compiler_params=_cparams(("arbitrary", "arbitrary"), 32),
        name="attn_core",
    )(sink, q, k, v, k, k, k, v, v, v)
    return o


def _lru_pre_kernel(x_ref, mod_ref, g_ref, w_ref, gate_ref, u_ref):
    mod = mod_ref[0]
    h = _rms_mod(x_ref[0], g_ref[...], mod[0:1], mod[1:2]).astype(BF16)
    y = _dot(h, w_ref[...])
    d = gate_ref.shape[-1]
    gate_ref[0] = jax.nn.gelu(y[:, :d]).astype(BF16)
    u_ref[0] = y[:, d:]


def _padded_copy(src_ref, pad_scr, n_ctx, pad):
    s = src_ref.shape[1]
    c = pad_scr.shape[1]
    zeros = jnp.zeros((pad, c), F32)
    pad_scr[0:pad] = zeros
    pad_scr[pad:pad + n_ctx] = src_ref[0, 0:n_ctx]
    pad_scr[pad + n_ctx:2 * pad + n_ctx] = zeros
    pad_scr[2 * pad + n_ctx:2 * pad + s] = src_ref[0, n_ctx:s]
    pad_scr[2 * pad + s:3 * pad + s] = zeros


def _padded_row(r, n_ctx, pad):
    return r + pad if r < n_ctx else r + 2 * pad


def _lru_seq_kernel(n_ctx, u_ref, gate_ref, cw_ref, cb_ref, wa_ref, ba_ref, wx_ref, bx_ref, lam_ref, y_ref,
                    pad_scr, a0_scr, b0_scr, a1_scr, b1_scr):
    s = u_ref.shape[1]
    pad = SUBLANES
    _padded_copy(u_ref, pad_scr, n_ctx, pad)
    a_scr = (a0_scr, a1_scr)
    b_scr = (b0_scr, b1_scr)
    cw = cw_ref[...]
    for ci in range(s // SEQ_CHUNK):
        r0 = ci * SEQ_CHUNK
        base = _padded_row(r0, n_ctx, pad) - LRU_CONV_LEFT
        u = cb_ref[...] + sum(pad_scr[base + k:base + k + SEQ_CHUNK] * cw[k:k + 1] for k in range(LRU_CONV))
        for d in range(2):
            r = jax.nn.sigmoid(_dot3(u, wa_ref[d, 0]) + ba_ref[d:d + 1])
            i = jax.nn.sigmoid(_dot3(u, wx_ref[d, 0]) + bx_ref[d:d + 1])
            log_a = -LRU_C * r * jax.nn.softplus(-lam_ref[d:d + 1])
            a = jnp.exp(log_a)
            a_scr[d][r0:r0 + SEQ_CHUNK] = a
            b_scr[d][r0:r0 + SEQ_CHUNK] = jnp.sqrt(jnp.tanh(-log_a) * (1.0 + a * a)) * (i * u)

    c = u_ref.shape[2]
    rid = lax.broadcasted_iota(jnp.int32, (SUBLANES, c), 0)
    n_steps = s // SUBLANES
    n_ctx_steps = n_ctx // SUBLANES

    def scan8(a, b, h_prev, reverse):
        for sh in (1, 2, 4):
            if reverse:
                keep = rid < SUBLANES - sh
                amt = SUBLANES - sh
            else:
                keep = rid >= sh
                amt = sh
            a_sh = jnp.where(keep, pltpu.roll(a, amt, 0), 1.0)
            b_sh = jnp.where(keep, pltpu.roll(b, amt, 0), 0.0)
            b = a * b_sh + b
            a = a * a_sh
        return b + a * h_prev

    def step(n, carry):
        hf, hb = carry
        rf = pl.multiple_of(n * SUBLANES, SUBLANES)
        nb = jnp.where(n < n_ctx_steps, n_ctx_steps - 1 - n, n_steps - 1 - n + n_ctx_steps)
        rb = pl.multiple_of(nb * SUBLANES, SUBLANES)
        out_f = scan8(a0_scr[pl.ds(rf, SUBLANES)], b0_scr[pl.ds(rf, SUBLANES)], hf, False)
        out_b = scan8(a1_scr[pl.ds(rb, SUBLANES)], b1_scr[pl.ds(rb, SUBLANES)], hb, True)
        b0_scr[pl.ds(rf, SUBLANES)] = out_f
        b1_scr[pl.ds(rb, SUBLANES)] = out_b
        return out_f[SUBLANES - 1:SUBLANES], out_b[0:1]

    zero = jnp.zeros((1, c), F32)
    lax.fori_loop(0, n_steps, step, (zero, zero))
    for ci in range(s // SEQ_CHUNK):
        rows = slice(ci * SEQ_CHUNK, (ci + 1) * SEQ_CHUNK)
        y_ref[0, rows] = ((b0_scr[rows] + b1_scr[rows]) * gate_ref[0, rows].astype(F32)).astype(BF16)


def _lru_layer(x, mod, norm_g, w_in, conv_w, conv_b, w_a, b_a, w_x, b_x, lam, n_ctx):
    n_batch, s, d = x.shape
    nt = s // TOKEN_TILE
    tile_spec = pl.BlockSpec((1, TOKEN_TILE, d), lambda b, j: (b, j, 0))
    gate, u = pl.pallas_call(
        _lru_pre_kernel,
        grid=(n_batch, nt),
        in_specs=[
            tile_spec,
            pl.BlockSpec((1, 6, d), _mod_row_map(n_batch)),
            pl.BlockSpec((1, d), lambda b, j: (0, 0)),
            pl.BlockSpec((d, 2 * d), lambda b, j: (0, 0)),
        ],
        out_specs=[tile_spec, tile_spec],
        out_shape=[jax.ShapeDtypeStruct((n_batch, s, d), BF16), jax.ShapeDtypeStruct((n_batch, s, d), F32)],
        compiler_params=_cparams(("arbitrary", "arbitrary"), 48),
        name="lru_in",
    )(x, mod, norm_g.reshape(1, d), w_in.astype(BF16))

    cblk = LRU_BLOCK
    ncb = d // cblk
    seq_spec = pl.BlockSpec((1, s, cblk), lambda b, c: (b, 0, c))
    vec2 = pl.BlockSpec((2, cblk), lambda b, c: (0, c))
    wspec = pl.BlockSpec((2, 1, cblk, cblk), lambda b, c: (0, c, 0, 0))
    seq_scr = pltpu.VMEM((s, cblk), F32)
    return pl.pallas_call(
        functools.partial(_lru_seq_kernel, n_ctx),
        grid=(n_batch, ncb),
        in_specs=[
            seq_spec, seq_spec,
            pl.BlockSpec((LRU_CONV, cblk), lambda b, c: (0, c)),
            pl.BlockSpec((1, cblk), lambda b, c: (0, c)),
            wspec, vec2, wspec, vec2, vec2,
        ],
        out_specs=seq_spec,
        out_shape=jax.ShapeDtypeStruct((n_batch, s, d), BF16),
        scratch_shapes=[pltpu.VMEM((s + 3 * SUBLANES, cblk), F32), seq_scr, seq_scr, seq_scr, seq_scr],
        compiler_params=_cparams(("arbitrary", "arbitrary"), 48),
        name="lru_scan",
    )(u, gate, conv_w, conv_b.reshape(1, d), w_a, b_a, w_x, b_x, lam)


def _conf_pre_kernel(x_ref, mod_ref, g_ref, w_ref, b_ref, o_ref):
    mod = mod_ref[0]
    h = _rms_mod(x_ref[0], g_ref[...], mod[0:1], mod[1:2]).astype(BF16)
    y = _dot(h, w_ref[...]) + b_ref[...]
    d = o_ref.shape[-1]
    o_ref[0] = y[:, :d] * jax.nn.sigmoid(y[:, d:])


def _conf_conv_kernel(n_ctx, u_ref, w_ref, b_ref, o_ref, pad_scr):
    s = u_ref.shape[1]
    pad = 2 * SUBLANES
    left = CONF_KERNEL // 2
    _padded_copy(u_ref, pad_scr, n_ctx, pad)
    w = w_ref[...]
    for ci in range(s // SEQ_CHUNK):
        r0 = ci * SEQ_CHUNK
        base = _padded_row(r0, n_ctx, pad) - left
        acc = b_ref[...] + pad_scr[base:base + SEQ_CHUNK] * w[0:1]
        for k in range(1, CONF_KERNEL):
            acc = acc + pad_scr[base + k:base + k + SEQ_CHUNK] * w[k:k + 1]
        o_ref[0, r0:r0 + SEQ_CHUNK] = acc


def _conf_layer(x, mod, norm_g, w_pw1, b_pw1, dw_w, dw_b, n_ctx):
    n_batch, s, d = x.shape
    nt = s // TOKEN_TILE
    tile_spec = pl.BlockSpec((1, TOKEN_TILE, d), lambda b, j: (b, j, 0))
    glu = pl.pallas_call(
        _conf_pre_kernel,
        grid=(n_batch, nt),
        in_specs=[
            tile_spec,
            pl.BlockSpec((1, 6, d), _mod_row_map(n_batch)),
            pl.BlockSpec((1, d), lambda b, j: (0, 0)),
            pl.BlockSpec((d, 2 * d), lambda b, j: (0, 0)),
            pl.BlockSpec((1, 2 * d), lambda b, j: (0, 0)),
        ],
        out_specs=tile_spec,
        out_shape=jax.ShapeDtypeStruct((n_batch, s, d), F32),
        compiler_params=_cparams(("arbitrary", "arbitrary"), 48),
        name="conf_in",
    )(x, mod, norm_g.reshape(1, d), w_pw1.astype(BF16), b_pw1.reshape(1, 2 * d))

    cblk = LANES
    seq_spec = pl.BlockSpec((1, s, cblk), lambda b, c: (b, 0, c))
    return pl.pallas_call(
        functools.partial(_conf_conv_kernel, n_ctx),
        grid=(n_batch, d // cblk),
        in_specs=[
            seq_spec,
            pl.BlockSpec((CONF_KERNEL, cblk), lambda b, c: (0, c)),
            pl.BlockSpec((1, cblk), lambda b, c: (0, c)),
        ],
        out_specs=seq_spec,
        out_shape=jax.ShapeDtypeStruct((n_batch, s, d), F32),
        scratch_shapes=[pltpu.VMEM((s + 6 * SUBLANES, cblk), F32)],
        compiler_params=_cparams(("arbitrary", "arbitrary"), 32),
        name="conf_dwconv",
    )(glu, dw_w, dw_b.reshape(1, d))


def _compare_exchange(planes, i, j):
    hi = jnp.maximum(planes[i], planes[j])
    lo = jnp.minimum(planes[i], planes[j])
    planes[i], planes[j] = hi, lo


def _batcher_pairs(n):
    pairs = []

    def merge(lo, m, r):
        step = 2 * r
        if step < m:
            merge(lo, m, step)
            merge(lo + r, m, step)
            for i in range(lo + r, lo + m - r, step):
                pairs.append((i, i + r))
        else:
            pairs.append((lo, lo + r))

    def sort(lo, m):
        if m > 1:
            half = m // 2
            sort(lo, half)
            sort(lo + half, half)
            merge(lo, m, 1)

    sort(0, n)
    return pairs


_SORT16 = _batcher_pairs(PEER_TOPK)


def _sort_desc(planes):
    planes = list(planes)
    for i, j in _SORT16:
        _compare_exchange(planes, i, j)
    return planes


def _merge_top(a, b):
    n = len(a)
    planes = [jnp.maximum(a[i], b[n - 1 - i]) for i in range(n)]
    d = n // 2
    while d >= 1:
        for i in range(n):
            if i & d == 0:
                _compare_exchange(planes, i, i + d)
        d //= 2
    return planes


def _top_sorted(groups):
    groups = [_sort_desc(g) for g in groups]
    while len(groups) > 1:
        groups = [_merge_top(groups[i], groups[i + 1]) for i in range(0, len(groups), 2)]
    return groups[0]


def _peer_stats(s1_scr, s2_scr, lane0, plane_scr, r2_ref, e2_ref, beta_ref, c1_ref):
    k = PEER_TOPK
    lanes = slice(lane0, lane0 + LANES)

    def top_planes(scr):
        groups = [[scr[pl.ds(g * k + i, PEER_HEADS, stride=N_KEYS), :] for i in range(k)]
                  for g in range(N_KEYS // k)]
        return _top_sorted(groups)

    v1 = top_planes(s1_scr)
    v2 = top_planes(s2_scr)
    cands = [v1[a] + v2[b] for a in range(k) for b in range(k) if (a + 1) * (b + 1) <= k]
    fill = jnp.full_like(v1[0], -jnp.inf)
    cands = cands + [fill] * (-len(cands) % k)
    top = _top_sorted([cands[i:i + k] for i in range(0, len(cands), k)])
    z = sum(jnp.exp(t - top[0]) for t in top)
    for b in range(k):
        plane_scr[b] = v2[b]
    plane_scr[k] = top[k - 1]
    plane_scr[k + 1] = v1[0]
    plane_scr[k + 2] = 0.5 / z
    part = N_KEYS // 2

    def head(h, carry):
        row = lambda i: plane_scr[i, pl.ds(h, 1), :]
        v2_h = [row(b) for b in range(k)]
        tau_h, top1_h, half_z_h = row(k), row(k + 1), row(k + 2)
        for p in range(0, N_KEYS, part):
            rows = pl.ds(pl.multiple_of(h * N_KEYS + p, part), part)
            s1 = s1_scr[rows, :]
            beta = sum(jnp.where(s1 + v2b >= tau_h, 1.0, 0.0) for v2b in v2_h)
            s2 = s2_scr[rows, :]
            rank2 = sum(jnp.where(v2b > s2, 1.0, 0.0) for v2b in v2_h)
            beta_ref[h, p:p + part, lanes] = _dup_bf16_words(beta)
            r2_ref[h, p // 2:(p + part) // 2, lanes] = _pack_rows(rank2)
            e2_ref[h, p // 2:(p + part) // 2, lanes] = _pack_rows(jnp.exp(s2 - v2_h[0]))
            c1_ref[h, p:p + part, lanes] = _dup_bf16_words(jnp.exp(s1 - top1_h) * half_z_h)
        return carry

    lax.fori_loop(0, PEER_HEADS, head, 0)


def _post_kernel(ln_silu, yin_ref, x_ref, mod_ref, w_ref, b_ref, ng_ref, lng_ref, lnb_ref, wk1_ref, wk2_ref,
                 x1_ref, h2t_ref, r2_ref, e2_ref, beta_ref, c1_ref, s1_scr, s2_scr, plane_scr):
    mod = mod_ref[0]
    yin = yin_ref[0]
    if ln_silu:
        mu = jnp.mean(yin, axis=-1, keepdims=True)
        cen = yin - mu
        var = jnp.mean(cen * cen, axis=-1, keepdims=True)
        t = cen * lax.rsqrt(var + EPS) * lng_ref[...] + lnb_ref[...]
        yin = t * jax.nn.sigmoid(t)
    y = _dot(yin.astype(BF16), w_ref[...]) + b_ref[...]
    x1 = x_ref[0] + mod[2:3] * y
    x1_ref[0] = x1
    h2 = _rms_mod(x1, ng_ref[...], mod[3:4], mod[4:5])
    h2t = h2.T.astype(BF16)
    h2t_ref[...] = _pack_rows(h2t)
    s1 = _dot(wk1_ref[...], h2t)
    s2 = _dot(wk2_ref[...], h2t)
    for lc in range(s1_scr.shape[0]):
        s1_scr[lc] = s1[:, lc * LANES:(lc + 1) * LANES]
        s2_scr[lc] = s2[:, lc * LANES:(lc + 1) * LANES]
    for lc in range(s1_scr.shape[0]):
        _peer_stats(s1_scr.at[lc], s2_scr.at[lc], lc * LANES, plane_scr.at[lc], r2_ref, e2_ref, beta_ref, c1_ref)


def _post_call(ln_silu, yin, x, mod, w, bias, norm_g2, ln_g, ln_b, wk1, wk2):
    n_batch, s, d = x.shape
    nt = s // TOKEN_TILE
    n_tok = n_batch * s
    tile_spec = pl.BlockSpec((1, TOKEN_TILE, d), lambda b, j: (b, j, 0))
    const = lambda shape: pl.BlockSpec(shape, lambda b, j: (0,) * len(shape))
    side_spec = pl.BlockSpec((PEER_HEADS, N_KEYS, TOKEN_TILE), lambda b, j: (0, 0, b * nt + j))
    side_shape = jax.ShapeDtypeStruct((PEER_HEADS, N_KEYS, n_tok), jnp.uint32)
    pair_spec = pl.BlockSpec((PEER_HEADS, N_KEYS // 2, TOKEN_TILE), lambda b, j: (0, 0, b * nt + j))
    pair_shape = jax.ShapeDtypeStruct((PEER_HEADS, N_KEYS // 2, n_tok), jnp.uint32)
    hk = PEER_HEADS * N_KEYS
    return pl.pallas_call(
        functools.partial(_post_kernel, ln_silu),
        grid=(n_batch, nt),
        in_specs=[
            tile_spec, tile_spec,
            pl.BlockSpec((1, 6, d), _mod_row_map(n_batch)),
            const((yin.shape[-1], d)), const((1, d)), const((1, d)), const((1, d)), const((1, d)),
            const((hk, d)), const((hk, d)),
        ],
        out_specs=[
            tile_spec,
            pl.BlockSpec((d // 2, TOKEN_TILE), lambda b, j: (0, b * nt + j)),
            pair_spec, pair_spec, side_spec, side_spec,
        ],
        out_shape=[
            jax.ShapeDtypeStruct((n_batch, s, d), F32),
            jax.ShapeDtypeStruct((d // 2, n_tok), jnp.uint32),
            pair_shape, pair_shape, side_shape, side_shape,
        ],
        scratch_shapes=[pltpu.VMEM((TOKEN_TILE // LANES, hk, LANES), F32)] * 2
        + [pltpu.VMEM((TOKEN_TILE // LANES, PEER_TOPK + 3, PEER_HEADS, LANES), F32)],
        compiler_params=_cparams(("arbitrary", "arbitrary"), 56),
        name="mixer_out_peer_stats",
    )(yin, x, mod, w.astype(BF16), bias.reshape(1, d), norm_g2.reshape(1, d), ln_g.reshape(1, d),
      ln_b.reshape(1, d), wk1, wk2)


def _peer_gate_block(tile, lane0, act_ref, g_ref, r2_ref, e2_ref, beta_ref, c1_ref):
    et = act_ref.shape[0]
    pack = 2 * SUBLANES
    n_i1 = et // N_KEYS

    def bcast(row, lanes):
        return _unpack_rows(jnp.broadcast_to(row[:, lanes], (SUBLANES, LANES)))[None]

    for cp in range(0, n_i1, 2):
        i1s = [jnp.clip(tile * n_i1 + c, 0, N_KEYS - 1) for c in (cp, cp + 1)]
        for lc in range(PEER_LANE_SPLIT // LANES):
            lanes = slice(lane0 + lc * LANES, lane0 + (lc + 1) * LANES)
            gates = [None for _ in i1s]
            for h in range(PEER_HEADS):
                r2 = _unpack_rows(r2_ref[h, :, lanes]).reshape(N_KEYS // pack, pack, LANES)
                e2 = _unpack_rows(e2_ref[h, :, lanes]).reshape(N_KEYS // pack, pack, LANES)
                for n, i1 in enumerate(i1s):
                    beta = bcast(beta_ref[h, pl.ds(i1, 1), :], lanes)
                    c1 = bcast(c1_ref[h, pl.ds(i1, 1), :], lanes)
                    term = jnp.where(r2 < beta, e2 * c1, jnp.zeros_like(e2))
                    gates[n] = term if h == 0 else gates[n] + term
            for n, c in enumerate((cp, cp + 1)):
                rows = slice(c * N_KEYS, (c + 1) * N_KEYS)
                x = act_ref[rows, lanes]
                t = jnp.tanh(x * (GELU_C0 + GELU_C1 * (x * x)))
                g_ref[rows, lanes] = (gates[n].reshape(N_KEYS, LANES) * x) * (1.0 + t)


def _peer_kernel(h2t_ref, r2_ref, e2_ref, beta_ref, c1_ref, u_ref, vt_ref, x1_ref, moda_ref, modb_ref, o_ref,
                 act0, act1, g0, g1, acc):
    k = pl.program_id(1)
    et, tt = act0.shape
    side = (r2_ref, e2_ref, beta_ref, c1_ref)

    @pl.when(k == 0)
    def _():
        act1[...] = jnp.zeros_like(act1)
        g0[...] = jnp.zeros_like(g0)
        g1[...] = jnp.zeros_like(g1)
        acc[...] = jnp.zeros_like(acc)

    splits = [slice(l, l + PEER_LANE_SPLIT) for l in range(0, tt, PEER_LANE_SPLIT)]
    hu = et // 2
    for sl in splits:
        act0[:, sl] = _dot(_unpack_rows(u_ref[0:hu]), _unpack_rows(h2t_ref[:, sl])).astype(BF16)
        acc[:, sl] += _dot(_unpack_rows(vt_ref[:, 0:et]), g0[:, sl])
        _peer_gate_block(2 * k - 1, sl.start, act1, g1, *side)
    for sl in splits:
        act1[:, sl] = _dot(_unpack_rows(u_ref[hu:2 * hu]), _unpack_rows(h2t_ref[:, sl])).astype(BF16)
        acc[:, sl] += _dot(_unpack_rows(vt_ref[:, et:2 * et]), g1[:, sl])
        _peer_gate_block(2 * k, sl.start, act0, g0, *side)

    @pl.when(k == pl.num_programs(1) - 1)
    def _():
        out = acc[...].T
        half = out.shape[0] // 2
        o_ref[0:half] = x1_ref[0:half] + moda_ref[0, 5:6] * out[0:half]
        o_ref[half:] = x1_ref[half:] + modb_ref[0, 5:6] * out[half:]


def _peer_tables_kernel(u_ref, v_ref, up_ref, vtp_ref):
    up_ref[0] = _pack_rows(u_ref[0])
    vtp_ref[0] = _pack_rows(v_ref[0].T)


def _peer_tables_call(peer_u, peer_v):
    depth, n_exp, d = peer_u.shape
    rows = PEER_EXPERT_TILE
    tab_spec = pl.BlockSpec((1, rows, d), lambda l, e: (l, e, 0))
    return pl.pallas_call(
        _peer_tables_kernel,
        grid=(depth, n_exp // rows),
        in_specs=[tab_spec, tab_spec],
        out_specs=[pl.BlockSpec((1, rows // 2, d), lambda l, e: (l, e, 0)),
                   pl.BlockSpec((1, d // 2, rows), lambda l, e: (l, 0, e))],
        out_shape=[jax.ShapeDtypeStruct((depth, n_exp // 2, d), jnp.uint32),
                   jax.ShapeDtypeStruct((depth, d // 2, n_exp), jnp.uint32)],
        compiler_params=_cparams(("arbitrary", "arbitrary"), 32),
        name="peer_pack_tables",
    )(peer_u, peer_v)


def _peer_call(h2t, r2, e2, beta, c1, u_pk, vt_pk, x1, mod, n_batch):
    d, n_tok = x1.shape[-1], h2t.shape[1]
    n_exp = vt_pk.shape[1]
    tt, et = PEER_TOKEN_TILE, PEER_EXPERT_TILE
    nt = (n_tok // n_batch) // TOKEN_TILE
    n_pairs = n_exp // (2 * et)
    side_spec = pl.BlockSpec((PEER_HEADS, N_KEYS, tt), lambda i, k: (0, 0, i))
    pair_spec = pl.BlockSpec((PEER_HEADS, N_KEYS // 2, tt), lambda i, k: (0, 0, i))

    def mod_map(half):
        def index(i, k):
            t = i * (tt // TOKEN_TILE) + half
            return (jnp.where(t % nt == 0, n_batch, t // nt), 0, 0)
        return index

    return pl.pallas_call(
        _peer_kernel,
        grid=(n_tok // tt, n_pairs + 1),
        in_specs=[
            pl.BlockSpec((d // 2, tt), lambda i, k: (0, i)),
            pair_spec, pair_spec, side_spec, side_spec,
            pl.BlockSpec((et, d), lambda i, k: (jnp.minimum(k, n_pairs - 1), 0)),
            pl.BlockSpec((d // 2, 2 * et), lambda i, k: (0, jnp.maximum(k - 1, 0))),
            pl.BlockSpec((tt, d), lambda i, k: (i, 0)),
            pl.BlockSpec((1, 6, d), mod_map(0)),
            pl.BlockSpec((1, 6, d), mod_map(1)),
        ],
        out_specs=pl.BlockSpec((tt, d), lambda i, k: (i, 0)),
        out_shape=jax.ShapeDtypeStruct((n_tok, d), F32),
        scratch_shapes=[pltpu.VMEM((et, tt), BF16)] * 4 + [pltpu.VMEM((d, tt), F32)],
        compiler_params=_cparams(("arbitrary", "arbitrary"), 56),
        name="peer_dense",
    )(h2t, r2, e2, beta, c1, u_pk, vt_pk, x1.reshape(n_tok, d), mod, mod)


def kernel(x, c, ctx, c_ctx, ada_w, ada_b, norm_g, attn_wqkv, attn_q_gain, attn_k_gain, attn_sink, attn_wo,
           lru_w_in, lru_conv_w, lru_conv_b, lru_w_a, lru_b_a, lru_w_x, lru_b_x, lru_lambda, lru_w_out,
           conf_w_pw1, conf_b_pw1, conf_dw_w, conf_dw_b, conf_ln_g, conf_ln_b, conf_w_pw2, conf_b_pw2,
           peer_wq, peer_keys1, peer_keys2, peer_u, peer_v):
    n_batch, n_latent, d = x.shape
    n_ctx = ctx.shape[1]
    depth = ada_w.shape[0]
    assert n_ctx == TOKEN_TILE and n_latent % TOKEN_TILE == 0
    assert (n_batch * (n_ctx + n_latent)) % PEER_TOKEN_TILE == 0

    xs = jnp.concatenate([ctx, x], axis=1)
    mod_rows = -(-(n_batch + 1) // SUBLANES) * SUBLANES
    cvec = jnp.concatenate([c, c_ctx[None], jnp.zeros((mod_rows - n_batch - 1, d), F32)], axis=0)
    mods = _ada_call(cvec, ada_w, ada_b).reshape(depth, mod_rows, 6, d)
    wk1_all, wk2_all = _wk_call(peer_wq, peer_keys1, peer_keys2)
    u_pk_all, vt_pk_all = _peer_tables_call(peer_u, peer_v)
    rope_tabs = _rope_tables(n_ctx, n_latent)
    zeros_d = jnp.zeros((d,), F32)
    ones_d = jnp.ones((d,), F32)

    for layer in range(depth):
        kind, slot = layer % 3, layer // 3
        mod = mods[layer]
        if kind == 0:
            yin = _attn_layer(xs, mod, norm_g[layer, 0], attn_wqkv[slot], attn_q_gain[slot], attn_k_gain[slot],
                              attn_sink[slot], rope_tabs, n_ctx)
            post = (False, yin, xs, mod, attn_wo[slot], zeros_d, norm_g[layer, 1], ones_d, zeros_d)
        elif kind == 1:
            yin = _lru_layer(xs, mod, norm_g[layer, 0], lru_w_in[slot], lru_conv_w[slot], lru_conv_b[slot],
                             lru_w_a[slot], lru_b_a[slot], lru_w_x[slot], lru_b_x[slot], lru_lambda[slot], n_ctx)
            post = (False, yin, xs, mod, lru_w_out[slot], zeros_d, norm_g[layer, 1], ones_d, zeros_d)
        else:
            yin = _conf_layer(xs, mod, norm_g[layer, 0], conf_w_pw1[slot], conf_b_pw1[slot], conf_dw_w[slot],
                              conf_dw_b[slot], n_ctx)
            post = (True, yin, xs, mod, conf_w_pw2[slot], conf_b_pw2[slot], norm_g[layer, 1], conf_ln_g[slot],
                    conf_ln_b[slot])
        x1, h2t, r2, e2, beta, c1 = _post_call(*post, wk1_all[layer], wk2_all[layer])
        x2 = _peer_call(h2t, r2, e2, beta, c1, u_pk_all[layer], vt_pk_all[layer], x1, mod, n_batch)
        xs = x2.reshape(n_batch, n_ctx + n_latent, d)
    return xs[:, n_ctx:, :]
```

```python
import functools

import jax
import jax.numpy as jnp
from jax import lax
from jax.experimental import pallas as pl
from jax.experimental.pallas import tpu as pltpu

F32 = jnp.float32
BF16 = jnp.bfloat16

EPS = 1e-6
NEG_INF = -1e30
GRID_W = 64
N_HEADS = 16
N_KV_HEADS = 4
HEAD_DIM = 64
KV_GROUP = N_HEADS // N_KV_HEADS
WINDOW = 128
ATTN_BLOCK = 128
ROPE_THETA = 10000.0
LRU_BLOCK = 128
LRU_CONV = 4
LRU_CONV_LEFT = 2
LRU_C = 8.0
CONF_KERNEL = 31
N_KEYS = 128
PEER_HEADS = 8
PEER_TOPK = 16
GELU_C0 = 0.7978845608028654
GELU_C1 = GELU_C0 * 0.044715

SUBLANES = 8
LANES = 128

TOKEN_TILE = 256
PEER_TOKEN_TILE = 512
PEER_EXPERT_TILE = 512
PEER_I1_BLOCK = 1
PEER_LANE_SPLIT = 256
SEQ_CHUNK = 256
LRU_SCAN_CHANNELS = 256
MIB = 1024 * 1024


def _cparams(semantics, vmem_mib):
    return pltpu.CompilerParams(dimension_semantics=semantics, vmem_limit_bytes=vmem_mib * MIB)


def _pack_rows(x):
    return pltpu.bitcast(x.astype(BF16), jnp.uint32)


def _unpack_rows(w):
    return pltpu.bitcast(w, BF16)


def _dup_bf16_words(x):
    bits = pltpu.bitcast(x.astype(BF16).astype(F32), jnp.uint32)
    return bits | (bits >> 16)


def _dot(a, b):
    return jnp.dot(a, b, preferred_element_type=F32)


def _dot_nt(a, b):
    return lax.dot_general(a, b, (((1,), (1,)), ((), ())), preferred_element_type=F32)


def _split_bf16(x):
    hi = x.astype(BF16)
    lo = (x - hi.astype(F32)).astype(BF16)
    return hi, lo


def _dot3(x, w):
    xh, xl = _split_bf16(x)
    wh, wl = _split_bf16(w)
    return _dot(xh, wh) + _dot(xh, wl) + _dot(xl, wh)


def _rms_mod(x, g, shift, scale):
    ms = jnp.mean(x * x, axis=-1, keepdims=True)
    return (x * lax.rsqrt(ms + EPS) * g) * (1.0 + scale) + shift


def _mod_row_map(n_batch):
    return lambda b, j: (jnp.where(j == 0, n_batch, b), 0, 0)


def _ada_kernel(cv_ref, w_ref, b_ref, o_ref):
    cv = cv_ref[...]
    s = cv * jax.nn.sigmoid(cv)
    o_ref[0] = jnp.dot(s, w_ref[0], preferred_element_type=F32, precision=lax.Precision.HIGHEST) + b_ref[0]


def _ada_call(cvec, ada_w, ada_b):
    depth, d, d6 = ada_w.shape
    rows = cvec.shape[0]
    bn = 1024
    return pl.pallas_call(
        _ada_kernel,
        grid=(depth, d6 // bn),
        in_specs=[
            pl.BlockSpec((rows, d), lambda l, n: (0, 0)),
            pl.BlockSpec((1, d, bn), lambda l, n: (l, 0, n)),
            pl.BlockSpec((1, 1, bn), lambda l, n: (l, 0, n)),
        ],
        out_specs=pl.BlockSpec((1, rows, bn), lambda l, n: (l, 0, n)),
        out_shape=jax.ShapeDtypeStruct((depth, rows, d6), F32),
        compiler_params=_cparams(("arbitrary", "arbitrary"), 32),
        name="ada_mod",
    )(cvec, ada_w, ada_b.reshape(depth, 1, d6))


def _wk_kernel(wq_ref, k1_ref, k2_ref, o1_ref, o2_ref):
    w = wq_ref[0]
    half = N_KEYS
    hp = lax.Precision.HIGHEST
    dn = (((1,), (1,)), ((), ()))
    o1_ref[0] = lax.dot_general(k1_ref[0, 0], w[:, :half], dn, preferred_element_type=F32, precision=hp).astype(BF16)
    o2_ref[0] = lax.dot_general(k2_ref[0, 0], w[:, half:], dn, preferred_element_type=F32, precision=hp).astype(BF16)


def _wk_call(peer_wq, keys1, keys2):
    depth, d, _ = peer_wq.shape
    qd = 2 * N_KEYS
    out = jax.ShapeDtypeStruct((depth, PEER_HEADS * N_KEYS, d), BF16)
    return pl.pallas_call(
        _wk_kernel,
        grid=(depth, PEER_HEADS),
        in_specs=[
            pl.BlockSpec((1, d, qd), lambda l, h: (l, 0, h)),
            pl.BlockSpec((1, 1, N_KEYS, N_KEYS), lambda l, h: (l, h, 0, 0)),
            pl.BlockSpec((1, 1, N_KEYS, N_KEYS), lambda l, h: (l, h, 0, 0)),
        ],
        out_specs=[
            pl.BlockSpec((1, N_KEYS, d), lambda l, h: (l, h, 0)),
            pl.BlockSpec((1, N_KEYS, d), lambda l, h: (l, h, 0)),
        ],
        out_shape=[out, out],
        compiler_params=_cparams(("arbitrary", "arbitrary"), 32),
        name="peer_fold_keys",
    )(peer_wq, keys1, keys2)


def _attn_pre_kernel(x_ref, mod_ref, g_ref, w_ref, bd_ref, qg_ref, kg_ref, cos_ref, sm_ref, sp_ref,
                     q_ref, k_ref, v_ref):
    mod = mod_ref[0]
    h = _rms_mod(x_ref[0], g_ref[...], mod[0:1], mod[1:2]).astype(BF16)
    qkv = _dot(h, w_ref[...])
    nq = N_HEADS * HEAD_DIM
    nk = N_KV_HEADS * HEAD_DIM
    q, k, v = qkv[:, :nq], qkv[:, nq:nq + nk], qkv[:, nq + nk:]

    def head_norm(t, bd, gain):
        hi, lo = _split_bf16(t * t)
        ms = _dot(hi, bd) + _dot(lo, bd)
        return t * lax.rsqrt(ms + EPS) * gain

    def rope(t, width):
        reps = width // LANES
        cs = jnp.tile(cos_ref[...], (1, reps))
        sm = jnp.tile(sm_ref[...], (1, reps))
        sp = jnp.tile(sp_ref[...], (1, reps))
        quarter = HEAD_DIM // 4
        return t * cs + pltpu.roll(t, width - quarter, 1) * sm + pltpu.roll(t, quarter, 1) * sp

    qn = rope(head_norm(q, bd_ref[...], qg_ref[...]), nq)
    kn = rope(head_norm(k, bd_ref[0:nk, 0:nk], kg_ref[...]), nk)
    q_ref[0] = (qn * (HEAD_DIM ** -0.5)).astype(BF16)
    k_ref[0] = kn.astype(BF16)
    v_ref[0] = v.astype(BF16)


def _attn_kernel(n_latent, sink_ref, q_ref, kc_ref, vc_ref, kp_ref, kcur_ref, kn_ref, vp_ref, vcur_ref, vn_ref,
                 o_ref):
    j = pl.program_id(1)
    n_ctx = kc_ref.shape[1]
    blk = ATTN_BLOCK
    bi = j - n_ctx // blk
    q = q_ref[0]
    kcat = jnp.concatenate([kc_ref[0], kp_ref[0], kcur_ref[0], kn_ref[0]], axis=0)
    vcat = jnp.concatenate([vc_ref[0], vp_ref[0], vcur_ref[0], vn_ref[0]], axis=0)
    nk = n_ctx + 3 * blk
    rows = KV_GROUP * blk
    col = lax.broadcasted_iota(jnp.int32, (rows, nk), 1)
    qpos = bi * blk + lax.broadcasted_iota(jnp.int32, (rows, nk), 0) % blk
    kpos = (bi - 1) * blk + (col - n_ctx)
    in_window = (jnp.abs(qpos - kpos) <= WINDOW) & (kpos >= 0) & (kpos < n_latent) & (bi >= 0)
    valid = (col < n_ctx) | in_window
    ones = jnp.ones((nk, HEAD_DIM), BF16)
    for g in range(N_KV_HEADS):
        heads = range(g * KV_GROUP, (g + 1) * KV_GROUP)
        gs = slice(g * HEAD_DIM, (g + 1) * HEAD_DIM)
        qg = jnp.concatenate([q[:, h * HEAD_DIM:(h + 1) * HEAD_DIM] for h in heads], axis=0)
        sink = jnp.concatenate([jnp.full((blk, 1), sink_ref[h], F32) for h in heads], axis=0)
        s = jnp.where(valid, _dot_nt(qg, kcat[:, gs]), NEG_INF)
        m = jnp.maximum(jnp.max(s, axis=-1, keepdims=True), sink)
        p = jnp.exp(s - m).astype(BF16)
        oa = _dot(p, jnp.concatenate([vcat[:, gs], ones], axis=1))
        o = oa[:, :HEAD_DIM] / (oa[:, HEAD_DIM:HEAD_DIM + 1] + jnp.exp(sink - m))
        for i, h in enumerate(heads):
            o_ref[0, :, h * HEAD_DIM:(h + 1) * HEAD_DIM] = o[i * blk:(i + 1) * blk].astype(BF16)


def _rope_tables(n_ctx, n_latent):
    rows = n_latent // GRID_W
    row = jnp.repeat(jnp.arange(rows), GRID_W).astype(F32)
    col = jnp.tile(jnp.arange(GRID_W), rows).astype(F32)
    n_freq = HEAD_DIM // 4
    freqs = ROPE_THETA ** (-jnp.arange(n_freq, dtype=F32) / n_freq)
    ang = jnp.stack([row[:, None] * freqs, col[:, None] * freqs], axis=1)
    cos, sin = jnp.cos(ang), jnp.sin(ang)
    zero = jnp.zeros_like(sin)
    cs = jnp.stack([cos, cos], axis=2).reshape(n_latent, HEAD_DIM)
    sm = jnp.stack([-sin, zero], axis=2).reshape(n_latent, HEAD_DIM)
    sp = jnp.stack([zero, sin], axis=2).reshape(n_latent, HEAD_DIM)

    def full(tab, ctx_val):
        tab = jnp.concatenate([jnp.full((n_ctx, HEAD_DIM), ctx_val, F32), tab], axis=0)
        return jnp.tile(tab, (1, LANES // HEAD_DIM))

    return full(cs, 1.0), full(sm, 0.0), full(sp, 0.0)


def _attn_layer(x, mod, norm_g, w_qkv, q_gain, k_gain, sink, rope_tabs, n_ctx):
    n_batch, s, d = x.shape
    nt = s // TOKEN_TILE
    nq = N_HEADS * HEAD_DIM
    nk = N_KV_HEADS * HEAD_DIM
    eye = jnp.kron(jnp.eye(N_HEADS, dtype=F32), jnp.full((HEAD_DIM, HEAD_DIM), 1.0 / HEAD_DIM, F32)).astype(BF16)
    cs, sm, sp = rope_tabs
    tile_spec = lambda width: pl.BlockSpec((1, TOKEN_TILE, width), lambda b, j: (b, j, 0))
    const = lambda shape: pl.BlockSpec(shape, lambda b, j: (0,) * len(shape))
    tab_spec = pl.BlockSpec((TOKEN_TILE, LANES), lambda b, j: (j, 0))
    q, k, v = pl.pallas_call(
        _attn_pre_kernel,
        grid=(n_batch, nt),
        in_specs=[
            tile_spec(d),
            pl.BlockSpec((1, 6, d), _mod_row_map(n_batch)),
            const((1, d)),
            const((d, nq + 2 * nk)),
            const((nq, nq)),
            const((1, nq)),
            const((1, nk)),
            tab_spec, tab_spec, tab_spec,
        ],
        out_specs=[tile_spec(nq), tile_spec(nk), tile_spec(nk)],
        out_shape=[jax.ShapeDtypeStruct((n_batch, s, nq), BF16),
                   jax.ShapeDtypeStruct((n_batch, s, nk), BF16),
                   jax.ShapeDtypeStruct((n_batch, s, nk), BF16)],
        compiler_params=_cparams(("arbitrary", "arbitrary"), 48),
        name="attn_qkv",
    )(x, mod, norm_g.reshape(1, d), w_qkv.astype(BF16), eye,
      jnp.tile(q_gain, N_HEADS).reshape(1, nq), jnp.tile(k_gain, N_KV_HEADS).reshape(1, nk), cs, sm, sp)

    nb = s // ATTN_BLOCK
    blk_spec = lambda shift: pl.BlockSpec(
        (1, ATTN_BLOCK, nk), lambda b, j: (b, jnp.clip(j + shift, 0, nb - 1), 0))
    ctx_spec = pl.BlockSpec((1, n_ctx, nk), lambda b, j: (b, 0, 0))
    o = pl.pallas_call(
        functools.partial(_attn_kernel, s - n_ctx),
        grid=(n_batch, nb),
        in_specs=[
            pl.BlockSpec(memory_space=pltpu.SMEM),
            pl.BlockSpec((1, ATTN_BLOCK, nq), lambda b, j: (b, j, 0)),
            ctx_spec, ctx_spec,
            blk_spec(-1), blk_spec(0), blk_spec(1),
            blk_spec(-1), blk_spec(0), blk_spec(1),
        ],
        out_specs=pl.BlockSpec((1, ATTN_BLOCK, nq), lambda b, j: (b, j, 0)),
        out_shape=jax.ShapeDtypeStruct((n_batch, s, nq), BF16),
        compiler_params=_cparams(("arbitrary", "arbitrary"), 32),
        name="attn_core",
    )(sink, q, k, v, k, k, k, v, v, v)
    return o


def _lru_pre_kernel(x_ref, mod_ref, g_ref, w_ref, gate_ref, u_ref):
    mod = mod_ref[0]
    h = _rms_mod(x_ref[0], g_ref[...], mod[0:1], mod[1:2]).astype(BF16)
    y = _dot(h, w_ref[...])
    d = gate_ref.shape[-1]
    gate_ref[0] = jax.nn.gelu(y[:, :d]).astype(BF16)
    u_ref[0] = y[:, d:]


def _padded_copy(src_ref, pad_scr, n_ctx, pad):
    s = src_ref.shape[1]
    c = pad_scr.shape[1]
    zeros = jnp.zeros((pad, c), F32)
    pad_scr[0:pad] = zeros
    pad_scr[pad:pad + n_ctx] = src_ref[0, 0:n_ctx]
    pad_scr[pad + n_ctx:2 * pad + n_ctx] = zeros
    pad_scr[2 * pad + n_ctx:2 * pad + s] = src_ref[0, n_ctx:s]
    pad_scr[2 * pad + s:3 * pad + s] = zeros


def _padded_row(r, n_ctx, pad):
    return r + pad if r < n_ctx else r + 2 * pad


def _lru_seq_kernel(n_ctx, u_ref, gate_ref, cw_ref, cb_ref, wa_ref, ba_ref, wx_ref, bx_ref, lam_ref, y_ref,
                    pad_scr, a0_scr, b0_scr, a1_scr, b1_scr):
    s, c = u_ref.shape[1], u_ref.shape[2]
    blocks = [slice(i, i + LRU_BLOCK) for i in range(0, c, LRU_BLOCK)]
    pad = SUBLANES
    _padded_copy(u_ref, pad_scr, n_ctx, pad)
    a_scr = (a0_scr, a1_scr)
    b_scr = (b0_scr, b1_scr)
    cw = cw_ref[...]
    w_split = [[[_split_bf16(ref[d, n]) for n in range(len(blocks))] for ref in (wa_ref, wx_ref)]
               for d in range(2)]
    decay = [-LRU_C * jax.nn.softplus(-lam_ref[d:d + 1]) for d in range(2)]

    def gate_matmul(u_split, w):
        return jnp.concatenate(
            [_dot(xh, wh) + _dot(xh, wl) + _dot(xl, wh) for (xh, xl), (wh, wl) in zip(u_split, w)], axis=1)

    for ci in range(s // SEQ_CHUNK):
        r0 = ci * SEQ_CHUNK
        base = _padded_row(r0, n_ctx, pad) - LRU_CONV_LEFT
        u = cb_ref[...] + sum(pad_scr[base + k:base + k + SEQ_CHUNK] * cw[k:k + 1] for k in range(LRU_CONV))
        u_split = [_split_bf16(u[:, blk]) for blk in blocks]
        for d in range(2):
            r = jax.nn.sigmoid(gate_matmul(u_split, w_split[d][0]) + ba_ref[d:d + 1])
            i = jax.nn.sigmoid(gate_matmul(u_split, w_split[d][1]) + bx_ref[d:d + 1])
            log_a = r * decay[d]
            a = jnp.exp(log_a)
            a_scr[d][r0:r0 + SEQ_CHUNK] = a
            b_scr[d][r0:r0 + SEQ_CHUNK] = jnp.sqrt(jnp.tanh(-log_a) * (1.0 + a * a)) * (i * u)

    rid = lax.broadcasted_iota(jnp.int32, (SUBLANES, c), 0)
    n_steps = s // SUBLANES
    n_ctx_steps = n_ctx // SUBLANES

    def scan8(a, b, h_prev, reverse):
        for sh in (1, 2, 4):
            if reverse:
                keep = rid < SUBLANES - sh
                amt = SUBLANES - sh
            else:
                keep = rid >= sh
                amt = sh
            a_sh = jnp.where(keep, pltpu.roll(a, amt, 0), 1.0)
            b_sh = jnp.where(keep, pltpu.roll(b, amt, 0), 0.0)
            b = a * b_sh + b
            a = a * a_sh
        return b + a * h_prev

    def step(n, carry):
        hf, hb = carry
        rf = pl.multiple_of(n * SUBLANES, SUBLANES)
        nb = jnp.where(n < n_ctx_steps, n_ctx_steps - 1 - n, n_steps - 1 - n + n_ctx_steps)
        rb = pl.multiple_of(nb * SUBLANES, SUBLANES)
        out_f = scan8(a0_scr[pl.ds(rf, SUBLANES)], b0_scr[pl.ds(rf, SUBLANES)], hf, False)
        out_b = scan8(a1_scr[pl.ds(rb, SUBLANES)], b1_scr[pl.ds(rb, SUBLANES)], hb, True)
        b0_scr[pl.ds(rf, SUBLANES)] = out_f
        b1_scr[pl.ds(rb, SUBLANES)] = out_b
        return out_f[SUBLANES - 1:SUBLANES], out_b[0:1]

    zero = jnp.zeros((1, c), F32)
    lax.fori_loop(0, n_steps, step, (zero, zero))
    for ci in range(s // SEQ_CHUNK):
        rows = slice(ci * SEQ_CHUNK, (ci + 1) * SEQ_CHUNK)
        y_ref[0, rows] = ((b0_scr[rows] + b1_scr[rows]) * gate_ref[0, rows].astype(F32)).astype(BF16)


def _lru_layer(x, mod, norm_g, w_in, conv_w, conv_b, w_a, b_a, w_x, b_x, lam, n_ctx):
    n_batch, s, d = x.shape
    nt = s // TOKEN_TILE
    tile_spec = pl.BlockSpec((1, TOKEN_TILE, d), lambda b, j: (b, j, 0))
    gate, u = pl.pallas_call(
        _lru_pre_kernel,
        grid=(n_batch, nt),
        in_specs=[
            tile_spec,
            pl.BlockSpec((1, 6, d), _mod_row_map(n_batch)),
            pl.BlockSpec((1, d), lambda b, j: (0, 0)),
            pl.BlockSpec((d, 2 * d), lambda b, j: (0, 0)),
        ],
        out_specs=[tile_spec, tile_spec],
        out_shape=[jax.ShapeDtypeStruct((n_batch, s, d), BF16), jax.ShapeDtypeStruct((n_batch, s, d), F32)],
        compiler_params=_cparams(("arbitrary", "arbitrary"), 48),
        name="lru_in",
    )(x, mod, norm_g.reshape(1, d), w_in.astype(BF16))

    cblk = LRU_SCAN_CHANNELS
    ncb = d // cblk
    seq_spec = pl.BlockSpec((1, s, cblk), lambda b, c: (b, 0, c))
    vec2 = pl.BlockSpec((2, cblk), lambda b, c: (0, c))
    wspec = pl.BlockSpec((2, cblk // LRU_BLOCK, LRU_BLOCK, LRU_BLOCK), lambda b, c: (0, c, 0, 0))
    seq_scr = pltpu.VMEM((s, cblk), F32)
    return pl.pallas_call(
        functools.partial(_lru_seq_kernel, n_ctx),
        grid=(n_batch, ncb),
        in_specs=[
            seq_spec, seq_spec,
            pl.BlockSpec((LRU_CONV, cblk), lambda b, c: (0, c)),
            pl.BlockSpec((1, cblk), lambda b, c: (0, c)),
            wspec, vec2, wspec, vec2, vec2,
        ],
        out_specs=seq_spec,
        out_shape=jax.ShapeDtypeStruct((n_batch, s, d), BF16),
        scratch_shapes=[pltpu.VMEM((s + 3 * SUBLANES, cblk), F32), seq_scr, seq_scr, seq_scr, seq_scr],
        compiler_params=_cparams(("arbitrary", "arbitrary"), 48),
        name="lru_scan",
    )(u, gate, conv_w, conv_b.reshape(1, d), w_a, b_a, w_x, b_x, lam)


def _conf_pre_kernel(x_ref, mod_ref, g_ref, w_ref, b_ref, o_ref):
    mod = mod_ref[0]
    h = _rms_mod(x_ref[0], g_ref[...], mod[0:1], mod[1:2]).astype(BF16)
    y = _dot(h, w_ref[...]) + b_ref[...]
    d = o_ref.shape[-1]
    o_ref[0] = y[:, :d] * jax.nn.sigmoid(y[:, d:])


def _conf_conv_kernel(n_ctx, u_ref, w_ref, b_ref, o_ref, pad_scr):
    s = u_ref.shape[1]
    pad = 2 * SUBLANES
    left = CONF_KERNEL // 2
    _padded_copy(u_ref, pad_scr, n_ctx, pad)
    w = w_ref[...]
    for ci in range(s // SEQ_CHUNK):
        r0 = ci * SEQ_CHUNK
        base = _padded_row(r0, n_ctx, pad) - left
        acc = b_ref[...] + pad_scr[base:base + SEQ_CHUNK] * w[0:1]
        for k in range(1, CONF_KERNEL):
            acc = acc + pad_scr[base + k:base + k + SEQ_CHUNK] * w[k:k + 1]
        o_ref[0, r0:r0 + SEQ_CHUNK] = acc


def _conf_layer(x, mod, norm_g, w_pw1, b_pw1, dw_w, dw_b, n_ctx):
    n_batch, s, d = x.shape
    nt = s // TOKEN_TILE
    tile_spec = pl.BlockSpec((1, TOKEN_TILE, d), lambda b, j: (b, j, 0))
    glu = pl.pallas_call(
        _conf_pre_kernel,
        grid=(n_batch, nt),
        in_specs=[
            tile_spec,
            pl.BlockSpec((1, 6, d), _mod_row_map(n_batch)),
            pl.BlockSpec((1, d), lambda b, j: (0, 0)),
            pl.BlockSpec((d, 2 * d), lambda b, j: (0, 0)),
            pl.BlockSpec((1, 2 * d), lambda b, j: (0, 0)),
        ],
        out_specs=tile_spec,
        out_shape=jax.ShapeDtypeStruct((n_batch, s, d), F32),
        compiler_params=_cparams(("arbitrary", "arbitrary"), 48),
        name="conf_in",
    )(x, mod, norm_g.reshape(1, d), w_pw1.astype(BF16), b_pw1.reshape(1, 2 * d))

    cblk = LANES
    seq_spec = pl.BlockSpec((1, s, cblk), lambda b, c: (b, 0, c))
    return pl.pallas_call(
        functools.partial(_conf_conv_kernel, n_ctx),
        grid=(n_batch, d // cblk),
        in_specs=[
            seq_spec,
            pl.BlockSpec((CONF_KERNEL, cblk), lambda b, c: (0, c)),
            pl.BlockSpec((1, cblk), lambda b, c: (0, c)),
        ],
        out_specs=seq_spec,
        out_shape=jax.ShapeDtypeStruct((n_batch, s, d), F32),
        scratch_shapes=[pltpu.VMEM((s + 6 * SUBLANES, cblk), F32)],
        compiler_params=_cparams(("arbitrary", "arbitrary"), 32),
        name="conf_dwconv",
    )(glu, dw_w, dw_b.reshape(1, d))


def _compare_exchange(planes, i, j):
    hi = jnp.maximum(planes[i], planes[j])
    lo = jnp.minimum(planes[i], planes[j])
    planes[i], planes[j] = hi, lo


def _batcher_pairs(n):
    pairs = []

    def merge(lo, m, r):
        step = 2 * r
        if step < m:
            merge(lo, m, step)
            merge(lo + r, m, step)
            for i in range(lo + r, lo + m - r, step):
                pairs.append((i, i + r))
        else:
            pairs.append((lo, lo + r))

    def sort(lo, m):
        if m > 1:
            half = m // 2
            sort(lo, half)
            sort(lo + half, half)
            merge(lo, m, 1)

    sort(0, n)
    return pairs


_SORT16 = _batcher_pairs(PEER_TOPK)


def _sort_desc(planes):
    planes = list(planes)
    for i, j in _SORT16:
        _compare_exchange(planes, i, j)
    return planes


def _merge_top(a, b):
    n = len(a)
    planes = [jnp.maximum(a[i], b[n - 1 - i]) for i in range(n)]
    d = n // 2
    while d >= 1:
        for i in range(n):
            if i & d == 0:
                _compare_exchange(planes, i, i + d)
        d //= 2
    return planes


def _top_sorted(groups):
    groups = [_sort_desc(g) for g in groups]
    while len(groups) > 1:
        groups = [_merge_top(groups[i], groups[i + 1]) for i in range(0, len(groups), 2)]
    return groups[0]


def _count_prefix(pred, vals):
    assert len(vals) == 16
    c8 = pred(vals[7])
    c4 = pred(jnp.where(c8, vals[11], vals[3]))
    lo, hi = jnp.where(c8, vals[9], vals[1]), jnp.where(c8, vals[13], vals[5])
    c2 = pred(jnp.where(c4, hi, lo))
    even = [jnp.where(c8, vals[8 + i], vals[i]) for i in (0, 2, 4, 6)]
    lo, hi = jnp.where(c4, even[2], even[0]), jnp.where(c4, even[3], even[1])
    c1 = pred(jnp.where(c2, hi, lo))
    count = (jnp.where(c8, 8.0, 0.0) + jnp.where(c4, 4.0, 0.0)) + (jnp.where(c2, 2.0, 0.0) + jnp.where(c1, 1.0, 0.0))
    return jnp.where(pred(vals[15]), 16.0, count)


def _peer_stats(s1_scr, s2_scr, lane0, plane_scr, r2_ref, e2_ref, beta_ref, c1_ref):
    k = PEER_TOPK
    lanes = slice(lane0, lane0 + LANES)

    def top_planes(scr):
        groups = [[scr[pl.ds(g * k + i, PEER_HEADS, stride=N_KEYS), :] for i in range(k)]
                  for g in range(N_KEYS // k)]
        return _top_sorted(groups)

    v1 = top_planes(s1_scr)
    v2 = top_planes(s2_scr)
    cands = [v1[a] + v2[b] for a in range(k) for b in range(k) if (a + 1) * (b + 1) <= k]
    fill = jnp.full_like(v1[0], -jnp.inf)
    cands = cands + [fill] * (-len(cands) % k)
    top = _top_sorted([cands[i:i + k] for i in range(0, len(cands), k)])
    z = sum(jnp.exp(t - top[0]) for t in top)
    for b in range(k):
        plane_scr[b] = v2[b]
    plane_scr[k] = top[k - 1]
    plane_scr[k + 1] = v1[0]
    plane_scr[k + 2] = 0.5 / z
    part = N_KEYS // 2

    def head(h, carry):
        row = lambda i: plane_scr[i, pl.ds(h, 1), :]
        v2_h = [row(b) for b in range(k)]
        tau_h, top1_h, half_z_h = row(k), row(k + 1), row(k + 2)
        for p in range(0, N_KEYS, part):
            rows = pl.ds(pl.multiple_of(h * N_KEYS + p, part), part)
            s1 = s1_scr[rows, :]
            beta = _count_prefix(lambda v: s1 + v >= tau_h, v2_h)
            s2 = s2_scr[rows, :]
            rank2 = _count_prefix(lambda v: v > s2, v2_h)
            beta_ref[h, p:p + part, lanes] = _dup_bf16_words(beta)
            r2_ref[h, p // 2:(p + part) // 2, lanes] = _pack_rows(rank2)
            e2_ref[h, p // 2:(p + part) // 2, lanes] = _pack_rows(jnp.exp(s2 - v2_h[0]))
            c1_ref[h, p:p + part, lanes] = _dup_bf16_words(jnp.exp(s1 - top1_h) * half_z_h)
        return carry

    lax.fori_loop(0, PEER_HEADS, head, 0)


def _post_kernel(ln_silu, yin_ref, x_ref, mod_ref, w_ref, b_ref, ng_ref, lng_ref, lnb_ref, wk1_ref, wk2_ref,
                 x1_ref, h2t_ref, r2_ref, e2_ref, beta_ref, c1_ref, s1_scr, s2_scr, plane_scr):
    mod = mod_ref[0]
    yin = yin_ref[0]
    if ln_silu:
        mu = jnp.mean(yin, axis=-1, keepdims=True)
        cen = yin - mu
        var = jnp.mean(cen * cen, axis=-1, keepdims=True)
        t = cen * lax.rsqrt(var + EPS) * lng_ref[...] + lnb_ref[...]
        yin = t * jax.nn.sigmoid(t)
    y = _dot(yin.astype(BF16), w_ref[...]) + b_ref[...]
    x1 = x_ref[0] + mod[2:3] * y
    x1_ref[0] = x1
    h2 = _rms_mod(x1, ng_ref[...], mod[3:4], mod[4:5])
    h2t = h2.T.astype(BF16)
    h2t_ref[...] = _pack_rows(h2t)
    s1 = _dot(wk1_ref[...], h2t)
    s2 = _dot(wk2_ref[...], h2t)
    for lc in range(s1_scr.shape[0]):
        s1_scr[lc] = s1[:, lc * LANES:(lc + 1) * LANES]
        s2_scr[lc] = s2[:, lc * LANES:(lc + 1) * LANES]
    for lc in range(s1_scr.shape[0]):
        _peer_stats(s1_scr.at[lc], s2_scr.at[lc], lc * LANES, plane_scr.at[lc], r2_ref, e2_ref, beta_ref, c1_ref)


def _post_call(ln_silu, yin, x, mod, w, bias, norm_g2, ln_g, ln_b, wk1_all, wk2_all, layer):
    n_batch, s, d = x.shape
    nt = s // TOKEN_TILE
    n_tok = n_batch * s
    tile_spec = pl.BlockSpec((1, TOKEN_TILE, d), lambda b, j: (b, j, 0))
    const = lambda shape: pl.BlockSpec(shape, lambda b, j: (0,) * len(shape))
    side_spec = pl.BlockSpec((PEER_HEADS, N_KEYS, TOKEN_TILE), lambda b, j: (0, 0, b * nt + j))
    side_shape = jax.ShapeDtypeStruct((PEER_HEADS, N_KEYS, n_tok), jnp.uint32)
    pair_spec = pl.BlockSpec((PEER_HEADS, N_KEYS // 2, TOKEN_TILE), lambda b, j: (0, 0, b * nt + j))
    pair_shape = jax.ShapeDtypeStruct((PEER_HEADS, N_KEYS // 2, n_tok), jnp.uint32)
    hk = PEER_HEADS * N_KEYS
    return pl.pallas_call(
        functools.partial(_post_kernel, ln_silu),
        grid=(n_batch, nt),
        in_specs=[
            tile_spec, tile_spec,
            pl.BlockSpec((1, 6, d), _mod_row_map(n_batch)),
            const((yin.shape[-1], d)), const((1, d)), const((1, d)), const((1, d)), const((1, d)),
            pl.BlockSpec((None, hk, d), lambda b, j: (layer, 0, 0)),
            pl.BlockSpec((None, hk, d), lambda b, j: (layer, 0, 0)),
        ],
        out_specs=[
            tile_spec,
            pl.BlockSpec((d // 2, TOKEN_TILE), lambda b, j: (0, b * nt + j)),
            pair_spec, pair_spec, side_spec, side_spec,
        ],
        out_shape=[
            jax.ShapeDtypeStruct((n_batch, s, d), F32),
            jax.ShapeDtypeStruct((d // 2, n_tok), jnp.uint32),
            pair_shape, pair_shape, side_shape, side_shape,
        ],
        scratch_shapes=[pltpu.VMEM((TOKEN_TILE // LANES, hk, LANES), F32)] * 2
        + [pltpu.VMEM((TOKEN_TILE // LANES, PEER_TOPK + 3, PEER_HEADS, LANES), F32)],
        compiler_params=_cparams(("arbitrary", "arbitrary"), 56),
        name="mixer_out_peer_stats",
    )(yin, x, mod, w.astype(BF16), bias.reshape(1, d), norm_g2.reshape(1, d), ln_g.reshape(1, d),
      ln_b.reshape(1, d), wk1_all, wk2_all)


def _peer_gate_block(tile, lane0, act_ref, g_ref, r2_ref, e2_ref, beta_ref, c1_ref):
    et = act_ref.shape[0]
    pack = 2 * SUBLANES
    n_i1 = et // N_KEYS

    def bcast(row, lanes):
        return _unpack_rows(jnp.broadcast_to(row[:, lanes], (SUBLANES, LANES)))[None]

    for cp in range(0, n_i1, PEER_I1_BLOCK):
        cs = tuple(range(cp, cp + PEER_I1_BLOCK))
        i1s = [jnp.clip(tile * n_i1 + c, 0, N_KEYS - 1) for c in cs]
        for lc in range(PEER_LANE_SPLIT // LANES):
            lanes = slice(lane0 + lc * LANES, lane0 + (lc + 1) * LANES)
            gates = [None for _ in i1s]
            for h in range(PEER_HEADS):
                r2 = _unpack_rows(r2_ref[h, :, lanes]).reshape(N_KEYS // pack, pack, LANES)
                e2 = _unpack_rows(e2_ref[h, :, lanes]).reshape(N_KEYS // pack, pack, LANES)
                for n, i1 in enumerate(i1s):
                    beta = bcast(beta_ref[h, pl.ds(i1, 1), :], lanes)
                    c1 = bcast(c1_ref[h, pl.ds(i1, 1), :], lanes)
                    term = jnp.where(r2 < beta, e2 * c1, jnp.zeros_like(e2))
                    gates[n] = term if h == 0 else gates[n] + term
            for n, c in enumerate(cs):
                rows = slice(c * N_KEYS, (c + 1) * N_KEYS)
                x = act_ref[rows, lanes]
                t = jnp.tanh(x * (GELU_C0 + GELU_C1 * (x * x)))
                g_ref[rows, lanes] = (gates[n].reshape(N_KEYS, LANES) * x) * (1.0 + t)


def _peer_kernel(h2t_ref, r2_ref, e2_ref, beta_ref, c1_ref, u_ref, vt_ref, x1_ref, moda_ref, modb_ref, o_ref,
                 act0, act1, g0, g1, acc):
    k = pl.program_id(1)
    et, tt = act0.shape
    side = (r2_ref, e2_ref, beta_ref, c1_ref)

    @pl.when(k == 0)
    def _():
        act1[...] = jnp.zeros_like(act1)
        g0[...] = jnp.zeros_like(g0)
        g1[...] = jnp.zeros_like(g1)
        acc[...] = jnp.zeros_like(acc)

    splits = [slice(l, l + PEER_LANE_SPLIT) for l in range(0, tt, PEER_LANE_SPLIT)]
    hu = et // 2
    last = pl.num_programs(1) - 1

    @pl.when(k < last)
    def _():
        for sl in splits:
            _peer_gate_block(2 * k - 1, sl.start, act1, g1, *side)
            act0[:, sl] = _dot(_unpack_rows(u_ref[0:hu]), _unpack_rows(h2t_ref[:, sl])).astype(BF16)
            acc[:, sl] += _dot(_unpack_rows(vt_ref[:, 0:et]), g0[:, sl])
        for sl in splits:
            _peer_gate_block(2 * k, sl.start, act0, g0, *side)
            act1[:, sl] = _dot(_unpack_rows(u_ref[hu:2 * hu]), _unpack_rows(h2t_ref[:, sl])).astype(BF16)
            acc[:, sl] += _dot(_unpack_rows(vt_ref[:, et:2 * et]), g1[:, sl])

    @pl.when(k == last)
    def _():
        for sl in splits:
            _peer_gate_block(2 * k - 1, sl.start, act1, g1, *side)
            acc[:, sl] += _dot(_unpack_rows(vt_ref[:, 0:et]), g0[:, sl])
        for sl in splits:
            acc[:, sl] += _dot(_unpack_rows(vt_ref[:, et:2 * et]), g1[:, sl])
        out = acc[...].T
        half = out.shape[0] // 2
        o_ref[0:half] = x1_ref[0:half] + moda_ref[0, 5:6] * out[0:half]
        o_ref[half:] = x1_ref[half:] + modb_ref[0, 5:6] * out[half:]


def _peer_tables_kernel(u_ref, v_ref, up_ref, vtp_ref):
    up_ref[0] = _pack_rows(u_ref[0])
    vtp_ref[0] = _pack_rows(v_ref[0].T)


def _peer_tables_call(peer_u, peer_v):
    depth, n_exp, d = peer_u.shape
    rows = PEER_EXPERT_TILE
    tab_spec = pl.BlockSpec((1, rows, d), lambda l, e: (l, e, 0))
    return pl.pallas_call(
        _peer_tables_kernel,
        grid=(depth, n_exp // rows),
        in_specs=[tab_spec, tab_spec],
        out_specs=[pl.BlockSpec((1, rows // 2, d), lambda l, e: (l, e, 0)),
                   pl.BlockSpec((1, d // 2, rows), lambda l, e: (l, 0, e))],
        out_shape=[jax.ShapeDtypeStruct((depth, n_exp // 2, d), jnp.uint32),
                   jax.ShapeDtypeStruct((depth, d // 2, n_exp), jnp.uint32)],
        compiler_params=_cparams(("arbitrary", "arbitrary"), 32),
        name="peer_pack_tables",
    )(peer_u, peer_v)


def _peer_call(h2t, r2, e2, beta, c1, u_pk_all, vt_pk_all, layer, x1, mod, n_batch):
    d, n_tok = x1.shape[-1], h2t.shape[1]
    n_exp = vt_pk_all.shape[2]
    tt, et = PEER_TOKEN_TILE, PEER_EXPERT_TILE
    nt = (n_tok // n_batch) // TOKEN_TILE
    n_pairs = n_exp // (2 * et)
    side_spec = pl.BlockSpec((PEER_HEADS, N_KEYS, tt), lambda i, k: (0, 0, i))
    pair_spec = pl.BlockSpec((PEER_HEADS, N_KEYS // 2, tt), lambda i, k: (0, 0, i))

    def mod_map(half):
        def index(i, k):
            t = i * (tt // TOKEN_TILE) + half
            return (jnp.where(t % nt == 0, n_batch, t // nt), 0, 0)
        return index

    return pl.pallas_call(
        _peer_kernel,
        grid=(n_tok // tt, n_pairs + 1),
        in_specs=[
            pl.BlockSpec((d // 2, tt), lambda i, k: (0, i)),
            pair_spec, pair_spec, side_spec, side_spec,
            pl.BlockSpec((None, et, d), lambda i, k: (layer, jnp.minimum(k, n_pairs - 1), 0)),
            pl.BlockSpec((None, d // 2, 2 * et), lambda i, k: (layer, 0, jnp.maximum(k - 1, 0))),
            pl.BlockSpec((tt, d), lambda i, k: (i, 0)),
            pl.BlockSpec((1, 6, d), mod_map(0)),
            pl.BlockSpec((1, 6, d), mod_map(1)),
        ],
        out_specs=pl.BlockSpec((tt, d), lambda i, k: (i, 0)),
        out_shape=jax.ShapeDtypeStruct((n_tok, d), F32),
        scratch_shapes=[pltpu.VMEM((et, tt), BF16)] * 4 + [pltpu.VMEM((d, tt), F32)],
        compiler_params=_cparams(("arbitrary", "arbitrary"), 56),
        name="peer_dense",
    )(h2t, r2, e2, beta, c1, u_pk_all, vt_pk_all, x1.reshape(n_tok, d), mod, mod)


def kernel(x, c, ctx, c_ctx, ada_w, ada_b, norm_g, attn_wqkv, attn_q_gain, attn_k_gain, attn_sink, attn_wo,
           lru_w_in, lru_conv_w, lru_conv_b, lru_w_a, lru_b_a, lru_w_x, lru_b_x, lru_lambda, lru_w_out,
           conf_w_pw1, conf_b_pw1, conf_dw_w, conf_dw_b, conf_ln_g, conf_ln_b, conf_w_pw2, conf_b_pw2,
           peer_wq, peer_keys1, peer_keys2, peer_u, peer_v):
    n_batch, n_latent, d = x.shape
    n_ctx = ctx.shape[1]
    depth = ada_w.shape[0]
    assert n_ctx == TOKEN_TILE and n_latent % TOKEN_TILE == 0
    assert (n_batch * (n_ctx + n_latent)) % PEER_TOKEN_TILE == 0

    xs = jnp.concatenate([ctx, x], axis=1)
    mod_rows = -(-(n_batch + 1) // SUBLANES) * SUBLANES
    cvec = jnp.concatenate([c, c_ctx[None], jnp.zeros((mod_rows - n_batch - 1, d), F32)], axis=0)
    mods = _ada_call(cvec, ada_w, ada_b).reshape(depth, mod_rows, 6, d)
    wk1_all, wk2_all = _wk_call(peer_wq, peer_keys1, peer_keys2)
    u_pk_all, vt_pk_all = _peer_tables_call(peer_u, peer_v)
    rope_tabs = _rope_tables(n_ctx, n_latent)
    zeros_d = jnp.zeros((d,), F32)
    ones_d = jnp.ones((d,), F32)

    for layer in range(depth):
        kind, slot = layer % 3, layer // 3
        mod = mods[layer]
        if kind == 0:
            yin = _attn_layer(xs, mod, norm_g[layer, 0], attn_wqkv[slot], attn_q_gain[slot], attn_k_gain[slot],
                              attn_sink[slot], rope_tabs, n_ctx)
            post = (False, yin, xs, mod, attn_wo[slot], zeros_d, norm_g[layer, 1], ones_d, zeros_d)
        elif kind == 1:
            yin = _lru_layer(xs, mod, norm_g[layer, 0], lru_w_in[slot], lru_conv_w[slot], lru_conv_b[slot],
                             lru_w_a[slot], lru_b_a[slot], lru_w_x[slot], lru_b_x[slot], lru_lambda[slot], n_ctx)
            post = (False, yin, xs, mod, lru_w_out[slot], zeros_d, norm_g[layer, 1], ones_d, zeros_d)
        else:
            yin = _conf_layer(xs, mod, norm_g[layer, 0], conf_w_pw1[slot], conf_b_pw1[slot], conf_dw_w[slot],
                              conf_dw_b[slot], n_ctx)
            post = (True, yin, xs, mod, conf_w_pw2[slot], conf_b_pw2[slot], norm_g[layer, 1], conf_ln_g[slot],
                    conf_ln_b[slot])
        x1, h2t, r2, e2, beta, c1 = _post_call(*post, wk1_all, wk2_all, layer)
        x2 = _peer_call(h2t, r2, e2, beta, c1, u_pk_all, vt_pk_all, layer, x1, mod, n_batch)
        xs = x2.reshape(n_batch, n_ctx + n_latent, d)
    return xs[:, n_ctx:, :]
```

```python
import functools

import jax
import jax.numpy as jnp
from jax import lax
from jax.experimental import pallas as pl
from jax.experimental.pallas import tpu as pltpu

F32 = jnp.float32
BF16 = jnp.bfloat16

EPS = 1e-6
NEG_INF = -1e30
GRID_W = 64
N_HEADS = 16
N_KV_HEADS = 4
HEAD_DIM = 64
KV_GROUP = N_HEADS // N_KV_HEADS
WINDOW = 128
ATTN_BLOCK = 128
ROPE_THETA = 10000.0
LRU_BLOCK = 128
LRU_CONV = 4
LRU_CONV_LEFT = 2
LRU_C = 8.0
CONF_KERNEL = 31
N_KEYS = 128
PEER_HEADS = 8
PEER_TOPK = 16
GELU_C0 = 0.7978845608028654
GELU_C1 = GELU_C0 * 0.044715

SUBLANES = 8
LANES = 128

TOKEN_TILE = 256
PEER_TOKEN_TILE = 1024
PEER_EXPERT_TILE = 512
PEER_I1_BLOCK = 1
PEER_LANE_SPLIT = 256
SEQ_CHUNK = 256
LRU_SCAN_CHANNELS = 256
MIB = 1024 * 1024


def _cparams(semantics, vmem_mib):
    return pltpu.CompilerParams(dimension_semantics=semantics, vmem_limit_bytes=vmem_mib * MIB)


def _pack_rows(x):
    return pltpu.bitcast(x.astype(BF16), jnp.uint32)


def _unpack_rows(w):
    return pltpu.bitcast(w, BF16)


def _dup_bf16_words(x):
    bits = pltpu.bitcast(x.astype(BF16).astype(F32), jnp.uint32)
    return bits | (bits >> 16)


def _dot(a, b):
    return jnp.dot(a, b, preferred_element_type=F32)


def _dot_nt(a, b):
    return lax.dot_general(a, b, (((1,), (1,)), ((), ())), preferred_element_type=F32)


def _split_bf16(x):
    hi = x.astype(BF16)
    lo = (x - hi.astype(F32)).astype(BF16)
    return hi, lo


def _dot3(x, w):
    xh, xl = _split_bf16(x)
    wh, wl = _split_bf16(w)
    return _dot(xh, wh) + _dot(xh, wl) + _dot(xl, wh)


def _rms_mod(x, g, shift, scale):
    ms = jnp.mean(x * x, axis=-1, keepdims=True)
    return (x * lax.rsqrt(ms + EPS) * g) * (1.0 + scale) + shift


def _mod_row_map(n_batch):
    return lambda b, j: (jnp.where(j == 0, n_batch, b), 0, 0)


def _ada_kernel(cv_ref, w_ref, b_ref, o_ref):
    cv = cv_ref[...]
    s = cv * jax.nn.sigmoid(cv)
    o_ref[0] = jnp.dot(s, w_ref[0], preferred_element_type=F32, precision=lax.Precision.HIGHEST) + b_ref[0]


def _ada_call(cvec, ada_w, ada_b):
    depth, d, d6 = ada_w.shape
    rows = cvec.shape[0]
    bn = 1024
    return pl.pallas_call(
        _ada_kernel,
        grid=(depth, d6 // bn),
        in_specs=[
            pl.BlockSpec((rows, d), lambda l, n: (0, 0)),
            pl.BlockSpec((1, d, bn), lambda l, n: (l, 0, n)),
            pl.BlockSpec((1, 1, bn), lambda l, n: (l, 0, n)),
        ],
        out_specs=pl.BlockSpec((1, rows, bn), lambda l, n: (l, 0, n)),
        out_shape=jax.ShapeDtypeStruct((depth, rows, d6), F32),
        compiler_params=_cparams(("arbitrary", "arbitrary"), 32),
        name="ada_mod",
    )(cvec, ada_w, ada_b.reshape(depth, 1, d6))


def _wk_kernel(wq_ref, k1_ref, k2_ref, o1_ref, o2_ref):
    w = wq_ref[0]
    half = N_KEYS
    hp = lax.Precision.HIGHEST
    dn = (((1,), (1,)), ((), ()))
    o1_ref[0] = lax.dot_general(k1_ref[0, 0], w[:, :half], dn, preferred_element_type=F32, precision=hp).astype(BF16)
    o2_ref[0] = lax.dot_general(k2_ref[0, 0], w[:, half:], dn, preferred_element_type=F32, precision=hp).astype(BF16)


def _wk_call(peer_wq, keys1, keys2):
    depth, d, _ = peer_wq.shape
    qd = 2 * N_KEYS
    out = jax.ShapeDtypeStruct((depth, PEER_HEADS * N_KEYS, d), BF16)
    return pl.pallas_call(
        _wk_kernel,
        grid=(depth, PEER_HEADS),
        in_specs=[
            pl.BlockSpec((1, d, qd), lambda l, h: (l, 0, h)),
            pl.BlockSpec((1, 1, N_KEYS, N_KEYS), lambda l, h: (l, h, 0, 0)),
            pl.BlockSpec((1, 1, N_KEYS, N_KEYS), lambda l, h: (l, h, 0, 0)),
        ],
        out_specs=[
            pl.BlockSpec((1, N_KEYS, d), lambda l, h: (l, h, 0)),
            pl.BlockSpec((1, N_KEYS, d), lambda l, h: (l, h, 0)),
        ],
        out_shape=[out, out],
        compiler_params=_cparams(("arbitrary", "arbitrary"), 32),
        name="peer_fold_keys",
    )(peer_wq, keys1, keys2)


def _attn_pre_kernel(x_ref, mod_ref, g_ref, w_ref, bd_ref, qg_ref, kg_ref, cos_ref, sm_ref, sp_ref,
                     q_ref, k_ref, v_ref):
    mod = mod_ref[0]
    h = _rms_mod(x_ref[0], g_ref[...], mod[0:1], mod[1:2]).astype(BF16)
    qkv = _dot(h, w_ref[...])
    nq = N_HEADS * HEAD_DIM
    nk = N_KV_HEADS * HEAD_DIM
    q, k, v = qkv[:, :nq], qkv[:, nq:nq + nk], qkv[:, nq + nk:]

    def head_norm(t, bd, gain):
        hi, lo = _split_bf16(t * t)
        ms = _dot(hi, bd) + _dot(lo, bd)
        return t * lax.rsqrt(ms + EPS) * gain

    def rope(t, width):
        reps = width // LANES
        cs = jnp.tile(cos_ref[...], (1, reps))
        sm = jnp.tile(sm_ref[...], (1, reps))
        sp = jnp.tile(sp_ref[...], (1, reps))
        quarter = HEAD_DIM // 4
        return t * cs + pltpu.roll(t, width - quarter, 1) * sm + pltpu.roll(t, quarter, 1) * sp

    qn = rope(head_norm(q, bd_ref[...], qg_ref[...]), nq)
    kn = rope(head_norm(k, bd_ref[0:nk, 0:nk], kg_ref[...]), nk)
    q_ref[0] = (qn * (HEAD_DIM ** -0.5)).astype(BF16)
    k_ref[0] = kn.astype(BF16)
    v_ref[0] = v.astype(BF16)


def _attn_kernel(n_latent, sink_ref, q_ref, kc_ref, vc_ref, kp_ref, kcur_ref, kn_ref, vp_ref, vcur_ref, vn_ref,
                 o_ref):
    j = pl.program_id(1)
    n_ctx = kc_ref.shape[1]
    blk = ATTN_BLOCK
    bi = j - n_ctx // blk
    q = q_ref[0]
    kcat = jnp.concatenate([kc_ref[0], kp_ref[0], kcur_ref[0], kn_ref[0]], axis=0)
    vcat = jnp.concatenate([vc_ref[0], vp_ref[0], vcur_ref[0], vn_ref[0]], axis=0)
    nk = n_ctx + 3 * blk
    rows = KV_GROUP * blk
    col = lax.broadcasted_iota(jnp.int32, (rows, nk), 1)
    qpos = bi * blk + lax.broadcasted_iota(jnp.int32, (rows, nk), 0) % blk
    kpos = (bi - 1) * blk + (col - n_ctx)
    in_window = (jnp.abs(qpos - kpos) <= WINDOW) & (kpos >= 0) & (kpos < n_latent) & (bi >= 0)
    valid = (col < n_ctx) | in_window
    ones = jnp.ones((nk, HEAD_DIM), BF16)
    for g in range(N_KV_HEADS):
        heads = range(g * KV_GROUP, (g + 1) * KV_GROUP)
        gs = slice(g * HEAD_DIM, (g + 1) * HEAD_DIM)
        qg = jnp.concatenate([q[:, h * HEAD_DIM:(h + 1) * HEAD_DIM] for h in heads], axis=0)
        sink = jnp.concatenate([jnp.full((blk, 1), sink_ref[h], F32) for h in heads], axis=0)
        s = jnp.where(valid, _dot_nt(qg, kcat[:, gs]), NEG_INF)
        m = jnp.maximum(jnp.max(s, axis=-1, keepdims=True), sink)
        p = jnp.exp(s - m).astype(BF16)
        oa = _dot(p, jnp.concatenate([vcat[:, gs], ones], axis=1))
        o = oa[:, :HEAD_DIM] / (oa[:, HEAD_DIM:HEAD_DIM + 1] + jnp.exp(sink - m))
        for i, h in enumerate(heads):
            o_ref[0, :, h * HEAD_DIM:(h + 1) * HEAD_DIM] = o[i * blk:(i + 1) * blk].astype(BF16)


def _rope_tables(n_ctx, n_latent):
    rows = n_latent // GRID_W
    row = jnp.repeat(jnp.arange(rows), GRID_W).astype(F32)
    col = jnp.tile(jnp.arange(GRID_W), rows).astype(F32)
    n_freq = HEAD_DIM // 4
    freqs = ROPE_THETA ** (-jnp.arange(n_freq, dtype=F32) / n_freq)
    ang = jnp.stack([row[:, None] * freqs, col[:, None] * freqs], axis=1)
    cos, sin = jnp.cos(ang), jnp.sin(ang)
    zero = jnp.zeros_like(sin)
    cs = jnp.stack([cos, cos], axis=2).reshape(n_latent, HEAD_DIM)
    sm = jnp.stack([-sin, zero], axis=2).reshape(n_latent, HEAD_DIM)
    sp = jnp.stack([zero, sin], axis=2).reshape(n_latent, HEAD_DIM)

    def full(tab, ctx_val):
        tab = jnp.concatenate([jnp.full((n_ctx, HEAD_DIM), ctx_val, F32), tab], axis=0)
        return jnp.tile(tab, (1, LANES // HEAD_DIM))

    return full(cs, 1.0), full(sm, 0.0), full(sp, 0.0)


def _attn_layer(x, mod, norm_g, w_qkv, q_gain, k_gain, sink, rope_tabs, n_ctx):
    n_batch, s, d = x.shape
    nt = s // TOKEN_TILE
    nq = N_HEADS * HEAD_DIM
    nk = N_KV_HEADS * HEAD_DIM
    eye = jnp.kron(jnp.eye(N_HEADS, dtype=F32), jnp.full((HEAD_DIM, HEAD_DIM), 1.0 / HEAD_DIM, F32)).astype(BF16)
    cs, sm, sp = rope_tabs
    tile_spec = lambda width: pl.BlockSpec((1, TOKEN_TILE, width), lambda b, j: (b, j, 0))
    const = lambda shape: pl.BlockSpec(shape, lambda b, j: (0,) * len(shape))
    tab_spec = pl.BlockSpec((TOKEN_TILE, LANES), lambda b, j: (j, 0))
    q, k, v = pl.pallas_call(
        _attn_pre_kernel,
        grid=(n_batch, nt),
        in_specs=[
            tile_spec(d),
            pl.BlockSpec((1, 6, d), _mod_row_map(n_batch)),
            const((1, d)),
            const((d, nq + 2 * nk)),
            const((nq, nq)),
            const((1, nq)),
            const((1, nk)),
            tab_spec, tab_spec, tab_spec,
        ],
        out_specs=[tile_spec(nq), tile_spec(nk), tile_spec(nk)],
        out_shape=[jax.ShapeDtypeStruct((n_batch, s, nq), BF16),
                   jax.ShapeDtypeStruct((n_batch, s, nk), BF16),
                   jax.ShapeDtypeStruct((n_batch, s, nk), BF16)],
        compiler_params=_cparams(("arbitrary", "arbitrary"), 48),
        name="attn_qkv",
    )(x, mod, norm_g.reshape(1, d), w_qkv.astype(BF16), eye,
      jnp.tile(q_gain, N_HEADS).reshape(1, nq), jnp.tile(k_gain, N_KV_HEADS).reshape(1, nk), cs, sm, sp)

    nb = s // ATTN_BLOCK
    blk_spec = lambda shift: pl.BlockSpec(
        (1, ATTN_BLOCK, nk), lambda b, j: (b, jnp.clip(j + shift, 0, nb - 1), 0))
    ctx_spec = pl.BlockSpec((1, n_ctx, nk), lambda b, j: (b, 0, 0))
    o = pl.pallas_call(
        functools.partial(_attn_kernel, s - n_ctx),
        grid=(n_batch, nb),
        in_specs=[
            pl.BlockSpec(memory_space=pltpu.SMEM),
            pl.BlockSpec((1, ATTN_BLOCK, nq), lambda b, j: (b, j, 0)),
            ctx_spec, ctx_spec,
            blk_spec(-1), blk_spec(0), blk_spec(1),
            blk_spec(-1), blk_spec(0), blk_spec(1),
        ],
        out_specs=pl.BlockSpec((1, ATTN_BLOCK, nq), lambda b, j: (b, j, 0)),
        out_shape=jax.ShapeDtypeStruct((n_batch, s, nq), BF16),
        compiler_params=_cparams(("arbitrary", "arbitrary"), 32),
        name="attn_core",
    )(sink, q, k, v, k, k, k, v, v, v)
    return o


def _lru_pre_kernel(x_ref, mod_ref, g_ref, w_ref, gate_ref, u_ref):
    mod = mod_ref[0]
    h = _rms_mod(x_ref[0], g_ref[...], mod[0:1], mod[1:2]).astype(BF16)
    y = _dot(h, w_ref[...])
    d = gate_ref.shape[-1]
    gate_ref[0] = jax.nn.gelu(y[:, :d]).astype(BF16)
    u_ref[0] = y[:, d:]


def _padded_copy(src_ref, pad_scr, n_ctx, pad):
    s = src_ref.shape[1]
    c = pad_scr.shape[1]
    zeros = jnp.zeros((pad, c), F32)
    pad_scr[0:pad] = zeros
    pad_scr[pad:pad + n_ctx] = src_ref[0, 0:n_ctx]
    pad_scr[pad + n_ctx:2 * pad + n_ctx] = zeros
    pad_scr[2 * pad + n_ctx:2 * pad + s] = src_ref[0, n_ctx:s]
    pad_scr[2 * pad + s:3 * pad + s] = zeros


def _padded_row(r, n_ctx, pad):
    return r + pad if r < n_ctx else r + 2 * pad


def _lru_seq_kernel(n_ctx, u_ref, gate_ref, cw_ref, cb_ref, wa_ref, ba_ref, wx_ref, bx_ref, lam_ref, y_ref,
                    pad_scr, a0_scr, b0_scr, a1_scr, b1_scr):
    s, c = u_ref.shape[1], u_ref.shape[2]
    blocks = [slice(i, i + LRU_BLOCK) for i in range(0, c, LRU_BLOCK)]
    pad = SUBLANES
    _padded_copy(u_ref, pad_scr, n_ctx, pad)
    a_scr = (a0_scr, a1_scr)
    b_scr = (b0_scr, b1_scr)
    cw = cw_ref[...]
    w_split = [[[_split_bf16(ref[d, n]) for n in range(len(blocks))] for ref in (wa_ref, wx_ref)]
               for d in range(2)]
    decay = [-LRU_C * jax.nn.softplus(-lam_ref[d:d + 1]) for d in range(2)]

    def gate_matmul(u_split, w):
        return jnp.concatenate(
            [_dot(xh, wh) + _dot(xh, wl) + _dot(xl, wh) for (xh, xl), (wh, wl) in zip(u_split, w)], axis=1)

    for ci in range(s // SEQ_CHUNK):
        r0 = ci * SEQ_CHUNK
        base = _padded_row(r0, n_ctx, pad) - LRU_CONV_LEFT
        u = cb_ref[...] + sum(pad_scr[base + k:base + k + SEQ_CHUNK] * cw[k:k + 1] for k in range(LRU_CONV))
        u_split = [_split_bf16(u[:, blk]) for blk in blocks]
        for d in range(2):
            r = jax.nn.sigmoid(gate_matmul(u_split, w_split[d][0]) + ba_ref[d:d + 1])
            i = jax.nn.sigmoid(gate_matmul(u_split, w_split[d][1]) + bx_ref[d:d + 1])
            log_a = r * decay[d]
            a = jnp.exp(log_a)
            a_scr[d][r0:r0 + SEQ_CHUNK] = a
            b_scr[d][r0:r0 + SEQ_CHUNK] = jnp.sqrt(jnp.tanh(-log_a) * (1.0 + a * a)) * (i * u)

    rid = lax.broadcasted_iota(jnp.int32, (SUBLANES, c), 0)
    n_steps = s // SUBLANES
    n_ctx_steps = n_ctx // SUBLANES

    def scan8(a, b, h_prev, reverse):
        for sh in (1, 2, 4):
            if reverse:
                keep = rid < SUBLANES - sh
                amt = SUBLANES - sh
            else:
                keep = rid >= sh
                amt = sh
            a_sh = jnp.where(keep, pltpu.roll(a, amt, 0), 1.0)
            b_sh = jnp.where(keep, pltpu.roll(b, amt, 0), 0.0)
            b = a * b_sh + b
            a = a * a_sh
        return b + a * h_prev

    def step(n, carry):
        hf, hb = carry
        rf = pl.multiple_of(n * SUBLANES, SUBLANES)
        nb = jnp.where(n < n_ctx_steps, n_ctx_steps - 1 - n, n_steps - 1 - n + n_ctx_steps)
        rb = pl.multiple_of(nb * SUBLANES, SUBLANES)
        out_f = scan8(a0_scr[pl.ds(rf, SUBLANES)], b0_scr[pl.ds(rf, SUBLANES)], hf, False)
        out_b = scan8(a1_scr[pl.ds(rb, SUBLANES)], b1_scr[pl.ds(rb, SUBLANES)], hb, True)
        b0_scr[pl.ds(rf, SUBLANES)] = out_f
        b1_scr[pl.ds(rb, SUBLANES)] = out_b
        return out_f[SUBLANES - 1:SUBLANES], out_b[0:1]

    zero = jnp.zeros((1, c), F32)
    lax.fori_loop(0, n_steps, step, (zero, zero))
    for ci in range(s // SEQ_CHUNK):
        rows = slice(ci * SEQ_CHUNK, (ci + 1) * SEQ_CHUNK)
        y_ref[0, rows] = ((b0_scr[rows] + b1_scr[rows]) * gate_ref[0, rows].astype(F32)).astype(BF16)


def _lru_layer(x, mod, norm_g, w_in, conv_w, conv_b, w_a, b_a, w_x, b_x, lam, n_ctx):
    n_batch, s, d = x.shape
    nt = s // TOKEN_TILE
    tile_spec = pl.BlockSpec((1, TOKEN_TILE, d), lambda b, j: (b, j, 0))
    gate, u = pl.pallas_call(
        _lru_pre_kernel,
        grid=(n_batch, nt),
        in_specs=[
            tile_spec,
            pl.BlockSpec((1, 6, d), _mod_row_map(n_batch)),
            pl.BlockSpec((1, d), lambda b, j: (0, 0)),
            pl.BlockSpec((d, 2 * d), lambda b, j: (0, 0)),
        ],
        out_specs=[tile_spec, tile_spec],
        out_shape=[jax.ShapeDtypeStruct((n_batch, s, d), BF16), jax.ShapeDtypeStruct((n_batch, s, d), F32)],
        compiler_params=_cparams(("arbitrary", "arbitrary"), 48),
        name="lru_in",
    )(x, mod, norm_g.reshape(1, d), w_in.astype(BF16))

    cblk = LRU_SCAN_CHANNELS
    ncb = d // cblk
    seq_spec = pl.BlockSpec((1, s, cblk), lambda b, c: (b, 0, c))
    vec2 = pl.BlockSpec((2, cblk), lambda b, c: (0, c))
    wspec = pl.BlockSpec((2, cblk // LRU_BLOCK, LRU_BLOCK, LRU_BLOCK), lambda b, c: (0, c, 0, 0))
    seq_scr = pltpu.VMEM((s, cblk), F32)
    return pl.pallas_call(
        functools.partial(_lru_seq_kernel, n_ctx),
        grid=(n_batch, ncb),
        in_specs=[
            seq_spec, seq_spec,
            pl.BlockSpec((LRU_CONV, cblk), lambda b, c: (0, c)),
            pl.BlockSpec((1, cblk), lambda b, c: (0, c)),
            wspec, vec2, wspec, vec2, vec2,
        ],
        out_specs=seq_spec,
        out_shape=jax.ShapeDtypeStruct((n_batch, s, d), BF16),
        scratch_shapes=[pltpu.VMEM((s + 3 * SUBLANES, cblk), F32), seq_scr, seq_scr, seq_scr, seq_scr],
        compiler_params=_cparams(("arbitrary", "arbitrary"), 48),
        name="lru_scan",
    )(u, gate, conv_w, conv_b.reshape(1, d), w_a, b_a, w_x, b_x, lam)


def _conf_pre_kernel(x_ref, mod_ref, g_ref, w_ref, b_ref, o_ref):
    mod = mod_ref[0]
    h = _rms_mod(x_ref[0], g_ref[...], mod[0:1], mod[1:2]).astype(BF16)
    y = _dot(h, w_ref[...]) + b_ref[...]
    d = o_ref.shape[-1]
    o_ref[0] = y[:, :d] * jax.nn.sigmoid(y[:, d:])


def _conf_conv_kernel(n_ctx, u_ref, w_ref, b_ref, o_ref, pad_scr):
    s = u_ref.shape[1]
    pad = 2 * SUBLANES
    left = CONF_KERNEL // 2
    _padded_copy(u_ref, pad_scr, n_ctx, pad)
    w = w_ref[...]
    for ci in range(s // SEQ_CHUNK):
        r0 = ci * SEQ_CHUNK
        base = _padded_row(r0, n_ctx, pad) - left
        acc = b_ref[...] + pad_scr[base:base + SEQ_CHUNK] * w[0:1]
        for k in range(1, CONF_KERNEL):
            acc = acc + pad_scr[base + k:base + k + SEQ_CHUNK] * w[k:k + 1]
        o_ref[0, r0:r0 + SEQ_CHUNK] = acc


def _conf_layer(x, mod, norm_g, w_pw1, b_pw1, dw_w, dw_b, n_ctx):
    n_batch, s, d = x.shape
    nt = s // TOKEN_TILE
    tile_spec = pl.BlockSpec((1, TOKEN_TILE, d), lambda b, j: (b, j, 0))
    glu = pl.pallas_call(
        _conf_pre_kernel,
        grid=(n_batch, nt),
        in_specs=[
            tile_spec,
            pl.BlockSpec((1, 6, d), _mod_row_map(n_batch)),
            pl.BlockSpec((1, d), lambda b, j: (0, 0)),
            pl.BlockSpec((d, 2 * d), lambda b, j: (0, 0)),
            pl.BlockSpec((1, 2 * d), lambda b, j: (0, 0)),
        ],
        out_specs=tile_spec,
        out_shape=jax.ShapeDtypeStruct((n_batch, s, d), F32),
        compiler_params=_cparams(("arbitrary", "arbitrary"), 48),
        name="conf_in",
    )(x, mod, norm_g.reshape(1, d), w_pw1.astype(BF16), b_pw1.reshape(1, 2 * d))

    cblk = LANES
    seq_spec = pl.BlockSpec((1, s, cblk), lambda b, c: (b, 0, c))
    return pl.pallas_call(
        functools.partial(_conf_conv_kernel, n_ctx),
        grid=(n_batch, d // cblk),
        in_specs=[
            seq_spec,
            pl.BlockSpec((CONF_KERNEL, cblk), lambda b, c: (0, c)),
            pl.BlockSpec((1, cblk), lambda b, c: (0, c)),
        ],
        out_specs=seq_spec,
        out_shape=jax.ShapeDtypeStruct((n_batch, s, d), F32),
        scratch_shapes=[pltpu.VMEM((s + 6 * SUBLANES, cblk), F32)],
        compiler_params=_cparams(("arbitrary", "arbitrary"), 32),
        name="conf_dwconv",
    )(glu, dw_w, dw_b.reshape(1, d))


def _compare_exchange(planes, i, j):
    hi = jnp.maximum(planes[i], planes[j])
    lo = jnp.minimum(planes[i], planes[j])
    planes[i], planes[j] = hi, lo


def _batcher_pairs(n):
    pairs = []

    def merge(lo, m, r):
        step = 2 * r
        if step < m:
            merge(lo, m, step)
            merge(lo + r, m, step)
            for i in range(lo + r, lo + m - r, step):
                pairs.append((i, i + r))
        else:
            pairs.append((lo, lo + r))

    def sort(lo, m):
        if m > 1:
            half = m // 2
            sort(lo, half)
            sort(lo + half, half)
            merge(lo, m, 1)

    sort(0, n)
    return pairs


_SORT16 = _batcher_pairs(PEER_TOPK)


def _sort_desc(planes):
    planes = list(planes)
    for i, j in _SORT16:
        _compare_exchange(planes, i, j)
    return planes


def _merge_top(a, b):
    n = len(a)
    planes = [jnp.maximum(a[i], b[n - 1 - i]) for i in range(n)]
    d = n // 2
    while d >= 1:
        for i in range(n):
            if i & d == 0:
                _compare_exchange(planes, i, i + d)
        d //= 2
    return planes


def _top_sorted(groups):
    groups = [_sort_desc(g) for g in groups]
    while len(groups) > 1:
        groups = [_merge_top(groups[i], groups[i + 1]) for i in range(0, len(groups), 2)]
    return groups[0]


def _count_prefix(pred, vals):
    assert len(vals) == 16
    c8 = pred(vals[7])
    c4 = pred(jnp.where(c8, vals[11], vals[3]))
    lo, hi = jnp.where(c8, vals[9], vals[1]), jnp.where(c8, vals[13], vals[5])
    c2 = pred(jnp.where(c4, hi, lo))
    even = [jnp.where(c8, vals[8 + i], vals[i]) for i in (0, 2, 4, 6)]
    lo, hi = jnp.where(c4, even[2], even[0]), jnp.where(c4, even[3], even[1])
    c1 = pred(jnp.where(c2, hi, lo))
    count = (jnp.where(c8, 8.0, 0.0) + jnp.where(c4, 4.0, 0.0)) + (jnp.where(c2, 2.0, 0.0) + jnp.where(c1, 1.0, 0.0))
    return jnp.where(pred(vals[15]), 16.0, count)


def _peer_stats(s1_scr, s2_scr, lane0, plane_scr, r2_ref, e2_ref, beta_ref, c1_ref):
    k = PEER_TOPK
    lanes = slice(lane0, lane0 + LANES)

    def top_planes(scr):
        groups = [[scr[pl.ds(g * k + i, PEER_HEADS, stride=N_KEYS), :] for i in range(k)]
                  for g in range(N_KEYS // k)]
        return _top_sorted(groups)

    v1 = top_planes(s1_scr)
    v2 = top_planes(s2_scr)
    cands = [v1[a] + v2[b] for a in range(k) for b in range(k) if (a + 1) * (b + 1) <= k]
    fill = jnp.full_like(v1[0], -jnp.inf)
    cands = cands + [fill] * (-len(cands) % k)
    top = _top_sorted([cands[i:i + k] for i in range(0, len(cands), k)])
    z = sum(jnp.exp(t - top[0]) for t in top)
    for b in range(k):
        plane_scr[b] = v2[b]
    plane_scr[k] = top[k - 1]
    plane_scr[k + 1] = v1[0]
    plane_scr[k + 2] = 0.5 / z
    part = N_KEYS // 2

    def head(h, carry):
        row = lambda i: plane_scr[i, pl.ds(h, 1), :]
        v2_h = [row(b) for b in range(k)]
        tau_h, top1_h, half_z_h = row(k), row(k + 1), row(k + 2)
        for p in range(0, N_KEYS, part):
            rows = pl.ds(pl.multiple_of(h * N_KEYS + p, part), part)
            s1 = s1_scr[rows, :]
            beta = _count_prefix(lambda v: s1 + v >= tau_h, v2_h)
            s2 = s2_scr[rows, :]
            rank2 = _count_prefix(lambda v: v > s2, v2_h)
            beta_ref[h, p:p + part, lanes] = _dup_bf16_words(beta)
            r2_ref[h, p // 2:(p + part) // 2, lanes] = _pack_rows(rank2)
            e2_ref[h, p // 2:(p + part) // 2, lanes] = _pack_rows(jnp.exp(s2 - v2_h[0]))
            c1_ref[h, p:p + part, lanes] = _dup_bf16_words(jnp.exp(s1 - top1_h) * half_z_h)
        return carry

    lax.fori_loop(0, PEER_HEADS, head, 0)


def _post_kernel(ln_silu, yin_ref, x_ref, mod_ref, w_ref, b_ref, ng_ref, lng_ref, lnb_ref, wk1_ref, wk2_ref,
                 x1_ref, h2t_ref, r2_ref, e2_ref, beta_ref, c1_ref, s1_scr, s2_scr, plane_scr):
    mod = mod_ref[0]
    yin = yin_ref[0]
    if ln_silu:
        mu = jnp.mean(yin, axis=-1, keepdims=True)
        cen = yin - mu
        var = jnp.mean(cen * cen, axis=-1, keepdims=True)
        t = cen * lax.rsqrt(var + EPS) * lng_ref[...] + lnb_ref[...]
        yin = t * jax.nn.sigmoid(t)
    y = _dot(yin.astype(BF16), w_ref[...]) + b_ref[...]
    x1 = x_ref[0] + mod[2:3] * y
    x1_ref[0] = x1
    h2 = _rms_mod(x1, ng_ref[...], mod[3:4], mod[4:5])
    h2t = h2.T.astype(BF16)
    h2t_ref[...] = _pack_rows(h2t)
    s1 = _dot(wk1_ref[...], h2t)
    s2 = _dot(wk2_ref[...], h2t)
    for lc in range(s1_scr.shape[0]):
        s1_scr[lc] = s1[:, lc * LANES:(lc + 1) * LANES]
        s2_scr[lc] = s2[:, lc * LANES:(lc + 1) * LANES]
    for lc in range(s1_scr.shape[0]):
        _peer_stats(s1_scr.at[lc], s2_scr.at[lc], lc * LANES, plane_scr.at[lc], r2_ref, e2_ref, beta_ref, c1_ref)


def _post_call(ln_silu, yin, x, mod, w, bias, norm_g2, ln_g, ln_b, wk1_all, wk2_all, layer):
    n_batch, s, d = x.shape
    nt = s // TOKEN_TILE
    n_tok = n_batch * s
    tile_spec = pl.BlockSpec((1, TOKEN_TILE, d), lambda b, j: (b, j, 0))
    const = lambda shape: pl.BlockSpec(shape, lambda b, j: (0,) * len(shape))
    side_spec = pl.BlockSpec((PEER_HEADS, N_KEYS, TOKEN_TILE), lambda b, j: (0, 0, b * nt + j))
    side_shape = jax.ShapeDtypeStruct((PEER_HEADS, N_KEYS, n_tok), jnp.uint32)
    pair_spec = pl.BlockSpec((PEER_HEADS, N_KEYS // 2, TOKEN_TILE), lambda b, j: (0, 0, b * nt + j))
    pair_shape = jax.ShapeDtypeStruct((PEER_HEADS, N_KEYS // 2, n_tok), jnp.uint32)
    hk = PEER_HEADS * N_KEYS
    return pl.pallas_call(
        functools.partial(_post_kernel, ln_silu),
        grid=(n_batch, nt),
        in_specs=[
            tile_spec, tile_spec,
            pl.BlockSpec((1, 6, d), _mod_row_map(n_batch)),
            const((yin.shape[-1], d)), const((1, d)), const((1, d)), const((1, d)), const((1, d)),
            pl.BlockSpec((None, hk, d), lambda b, j: (layer, 0, 0)),
            pl.BlockSpec((None, hk, d), lambda b, j: (layer, 0, 0)),
        ],
        out_specs=[
            tile_spec,
            pl.BlockSpec((d // 2, TOKEN_TILE), lambda b, j: (0, b * nt + j)),
            pair_spec, pair_spec, side_spec, side_spec,
        ],
        out_shape=[
            jax.ShapeDtypeStruct((n_batch, s, d), F32),
            jax.ShapeDtypeStruct((d // 2, n_tok), jnp.uint32),
            pair_shape, pair_shape, side_shape, side_shape,
        ],
        scratch_shapes=[pltpu.VMEM((TOKEN_TILE // LANES, hk, LANES), F32)] * 2
        + [pltpu.VMEM((TOKEN_TILE // LANES, PEER_TOPK + 3, PEER_HEADS, LANES), F32)],
        compiler_params=_cparams(("arbitrary", "arbitrary"), 56),
        name="mixer_out_peer_stats",
    )(yin, x, mod, w.astype(BF16), bias.reshape(1, d), norm_g2.reshape(1, d), ln_g.reshape(1, d),
      ln_b.reshape(1, d), wk1_all, wk2_all)


def _peer_gate_block(row0, lane0, act_ref, g_ref, r2_ref, e2_ref, beta_ref, c1_ref):
    et = act_ref.shape[0]
    pack = 2 * SUBLANES
    n_i1 = et // N_KEYS

    def bcast(ref, h, r, lanes):
        return _unpack_rows(jnp.broadcast_to(ref[h, r:r + 1, lanes], (SUBLANES, LANES)))[None]

    for cp in range(0, n_i1, PEER_I1_BLOCK):
        cs = tuple(range(cp, cp + PEER_I1_BLOCK))
        for lc in range(PEER_LANE_SPLIT // LANES):
            lanes = slice(lane0 + lc * LANES, lane0 + (lc + 1) * LANES)
            gates = [None for _ in cs]
            for h in range(PEER_HEADS):
                r2 = _unpack_rows(r2_ref[h, :, lanes]).reshape(N_KEYS // pack, pack, LANES)
                e2 = _unpack_rows(e2_ref[h, :, lanes]).reshape(N_KEYS // pack, pack, LANES)
                for n, c in enumerate(cs):
                    beta = bcast(beta_ref, h, row0 + c, lanes)
                    c1 = bcast(c1_ref, h, row0 + c, lanes)
                    term = jnp.where(r2 < beta, e2 * c1, jnp.zeros_like(e2))
                    gates[n] = term if h == 0 else gates[n] + term
            for n, c in enumerate(cs):
                rows = slice(c * N_KEYS, (c + 1) * N_KEYS)
                x = act_ref[rows, lanes]
                t = jnp.tanh(x * (GELU_C0 + GELU_C1 * (x * x)))
                g_ref[rows, lanes] = (gates[n].reshape(N_KEYS, LANES) * x) * (1.0 + t)


def _peer_kernel(n_mod, h2t_ref, r2_ref, e2_ref, bprev_ref, bcur_ref, cprev_ref, ccur_ref,
                 u0_ref, u1_ref, vt0_ref, vt1_ref, x1_ref, *rest):
    mod_refs, (o_ref, act0, act1, g0, g1, acc) = rest[:n_mod], rest[n_mod:]
    k = pl.program_id(1)
    et, tt = act0.shape
    half_rows = SUBLANES // 2
    odd = (half_rows, act1, g1, r2_ref, e2_ref, bprev_ref, cprev_ref)
    even = (0, act0, g0, r2_ref, e2_ref, bcur_ref, ccur_ref)

    def gate(stage, sl):
        row0, act, g, *side = stage
        _peer_gate_block(row0, sl.start, act, g, *side)

    @pl.when(k == 0)
    def _():
        act1[...] = jnp.zeros_like(act1)
        g0[...] = jnp.zeros_like(g0)
        g1[...] = jnp.zeros_like(g1)
        acc[...] = jnp.zeros_like(acc)

    splits = [slice(l, l + PEER_LANE_SPLIT) for l in range(0, tt, PEER_LANE_SPLIT)]
    last = pl.num_programs(1) - 1

    @pl.when(k < last)
    def _():
        for sl in splits:
            gate(odd, sl)
            act0[:, sl] = _dot(_unpack_rows(u0_ref[...]), _unpack_rows(h2t_ref[:, sl])).astype(BF16)
            acc[:, sl] += _dot(_unpack_rows(vt0_ref[...]), g0[:, sl])
        for sl in splits:
            gate(even, sl)
            act1[:, sl] = _dot(_unpack_rows(u1_ref[...]), _unpack_rows(h2t_ref[:, sl])).astype(BF16)
            acc[:, sl] += _dot(_unpack_rows(vt1_ref[...]), g1[:, sl])

    @pl.when(k == last)
    def _():
        for sl in splits:
            gate(odd, sl)
            acc[:, sl] += _dot(_unpack_rows(vt0_ref[...]), g0[:, sl])
        for sl in splits:
            acc[:, sl] += _dot(_unpack_rows(vt1_ref[...]), g1[:, sl])
        part = tt // n_mod
        for n, mod_ref in enumerate(mod_refs):
            rows = slice(n * part, (n + 1) * part)
            o_ref[rows] = x1_ref[rows] + mod_ref[0, 5:6] * acc[:, rows].T


def _peer_tables_kernel(u_ref, v_ref, up_ref, vtp_ref):
    up_ref[0] = _pack_rows(u_ref[0])
    vtp_ref[0] = _pack_rows(v_ref[0].T)


def _peer_tables_call(peer_u, peer_v):
    depth, n_exp, d = peer_u.shape
    rows = PEER_EXPERT_TILE
    tab_spec = pl.BlockSpec((1, rows, d), lambda l, e: (l, e, 0))
    return pl.pallas_call(
        _peer_tables_kernel,
        grid=(depth, n_exp // rows),
        in_specs=[tab_spec, tab_spec],
        out_specs=[pl.BlockSpec((1, rows // 2, d), lambda l, e: (l, e, 0)),
                   pl.BlockSpec((1, d // 2, rows), lambda l, e: (l, 0, e))],
        out_shape=[jax.ShapeDtypeStruct((depth, n_exp // 2, d), jnp.uint32),
                   jax.ShapeDtypeStruct((depth, d // 2, n_exp), jnp.uint32)],
        compiler_params=_cparams(("arbitrary", "arbitrary"), 32),
        name="peer_pack_tables",
    )(peer_u, peer_v)


def _peer_call(h2t, r2, e2, beta, c1, u_pk_all, vt_pk_all, layer, x1, mod, n_batch):
    d, n_tok = x1.shape[-1], h2t.shape[1]
    n_exp = vt_pk_all.shape[2]
    tt, et = PEER_TOKEN_TILE, PEER_EXPERT_TILE
    nt = (n_tok // n_batch) // TOKEN_TILE
    n_pairs = n_exp // (2 * et)
    pair_spec = pl.BlockSpec((PEER_HEADS, N_KEYS // 2, tt), lambda i, k: (0, 0, i))
    n_mod = tt // TOKEN_TILE

    def mod_map(part):
        def index(i, k):
            t = i * n_mod + part
            return (jnp.where(t % nt == 0, n_batch, t // nt), 0, 0)
        return index

    n_tiles = 2 * n_pairs

    def i1_spec(behind):
        return pl.BlockSpec((PEER_HEADS, SUBLANES, tt),
                            lambda i, k: (0, jnp.clip(k - behind, 0, n_pairs - 1), i))

    def u_spec(ahead):
        return pl.BlockSpec((None, et // 2, d), lambda i, k: (layer, jnp.minimum(2 * k + ahead, n_tiles - 1), 0))

    def vt_spec(behind):
        return pl.BlockSpec((None, d // 2, et), lambda i, k: (layer, 0, jnp.maximum(2 * k - behind, 0)))

    x1_spec = pl.BlockSpec((tt, d), lambda i, k: (jnp.where(k == 0, jnp.maximum(i - 1, 0), i), 0))
    assert et // N_KEYS == SUBLANES // 2
    return pl.pallas_call(
        functools.partial(_peer_kernel, n_mod),
        grid=(n_tok // tt, n_pairs + 1),
        in_specs=[
            pl.BlockSpec((d // 2, tt), lambda i, k: (0, i)),
            pair_spec, pair_spec,
            i1_spec(1), i1_spec(0), i1_spec(1), i1_spec(0),
            u_spec(0), u_spec(1), vt_spec(2), vt_spec(1),
            x1_spec,
        ] + [pl.BlockSpec((1, 6, d), mod_map(part)) for part in range(n_mod)],
        out_specs=pl.BlockSpec((tt, d), lambda i, k: (i, 0)),
        out_shape=jax.ShapeDtypeStruct((n_tok, d), F32),
        scratch_shapes=[pltpu.VMEM((et, tt), BF16)] * 4 + [pltpu.VMEM((d, tt), F32)],
        compiler_params=_cparams(("arbitrary", "arbitrary"), 56),
        name="peer_dense",
    )(h2t, r2, e2, beta, beta, c1, c1, u_pk_all, u_pk_all, vt_pk_all, vt_pk_all, x1.reshape(n_tok, d),
      *([mod] * n_mod))


def kernel(x, c, ctx, c_ctx, ada_w, ada_b, norm_g, attn_wqkv, attn_q_gain, attn_k_gain, attn_sink, attn_wo,
           lru_w_in, lru_conv_w, lru_conv_b, lru_w_a, lru_b_a, lru_w_x, lru_b_x, lru_lambda, lru_w_out,
           conf_w_pw1, conf_b_pw1, conf_dw_w, conf_dw_b, conf_ln_g, conf_ln_b, conf_w_pw2, conf_b_pw2,
           peer_wq, peer_keys1, peer_keys2, peer_u, peer_v):
    n_batch, n_latent, d = x.shape
    n_ctx = ctx.shape[1]
    depth = ada_w.shape[0]
    assert n_ctx == TOKEN_TILE and n_latent % TOKEN_TILE == 0
    assert (n_batch * (n_ctx + n_latent)) % PEER_TOKEN_TILE == 0

    xs = jnp.concatenate([ctx, x], axis=1)
    mod_rows = -(-(n_batch + 1) // SUBLANES) * SUBLANES
    cvec = jnp.concatenate([c, c_ctx[None], jnp.zeros((mod_rows - n_batch - 1, d), F32)], axis=0)
    mods = _ada_call(cvec, ada_w, ada_b).reshape(depth, mod_rows, 6, d)
    wk1_all, wk2_all = _wk_call(peer_wq, peer_keys1, peer_keys2)
    u_pk_all, vt_pk_all = _peer_tables_call(peer_u, peer_v)
    rope_tabs = _rope_tables(n_ctx, n_latent)
    zeros_d = jnp.zeros((d,), F32)
    ones_d = jnp.ones((d,), F32)

    for layer in range(depth):
        kind, slot = layer % 3, layer // 3
        mod = mods[layer]
        if kind == 0:
            yin = _attn_layer(xs, mod, norm_g[layer, 0], attn_wqkv[slot], attn_q_gain[slot], attn_k_gain[slot],
                              attn_sink[slot], rope_tabs, n_ctx)
            post = (False, yin, xs, mod, attn_wo[slot], zeros_d, norm_g[layer, 1], ones_d, zeros_d)
        elif kind == 1:
            yin = _lru_layer(xs, mod, norm_g[layer, 0], lru_w_in[slot], lru_conv_w[slot], lru_conv_b[slot],
                             lru_w_a[slot], lru_b_a[slot], lru_w_x[slot], lru_b_x[slot], lru_lambda[slot], n_ctx)
            post = (False, yin, xs, mod, lru_w_out[slot], zeros_d, norm_g[layer, 1], ones_d, zeros_d)
        else:
            yin = _conf_layer(xs, mod, norm_g[layer, 0], conf_w_pw1[slot], conf_b_pw1[slot], conf_dw_w[slot],
                              conf_dw_b[slot], n_ctx)
            post = (True, yin, xs, mod, conf_w_pw2[slot], conf_b_pw2[slot], norm_g[layer, 1], conf_ln_g[slot],
                    conf_ln_b[slot])
        x1, h2t, r2, e2, beta, c1 = _post_call(*post, wk1_all, wk2_all, layer)
        x2 = _peer_call(h2t, r2, e2, beta, c1, u_pk_all, vt_pk_all, layer, x1, mod, n_batch)
        xs = x2.reshape(n_batch, n_ctx + n_latent, d)
    return xs[:, n_ctx:, :]
```

```python
import functools

import jax
import jax.numpy as jnp
from jax import lax
from jax.experimental import pallas as pl
from jax.experimental.pallas import tpu as pltpu

F32 = jnp.float32
BF16 = jnp.bfloat16

EPS = 1e-6
NEG_INF = -1e30
GRID_W = 64
N_HEADS = 16
N_KV_HEADS = 4
HEAD_DIM = 64
KV_GROUP = N_HEADS // N_KV_HEADS
WINDOW = 128
ATTN_BLOCK = 128
ROPE_THETA = 10000.0
LRU_BLOCK = 128
LRU_CONV = 4
LRU_CONV_LEFT = 2
LRU_C = 8.0
CONF_KERNEL = 31
N_KEYS = 128
PEER_HEADS = 8
PEER_TOPK = 16
GELU_C0 = 0.7978845608028654
GELU_C1 = GELU_C0 * 0.044715

SUBLANES = 8
LANES = 128

TOKEN_TILE = 256
PEER_TOKEN_TILE = 1024
PEER_EXPERT_TILE = 512
PEER_I1_BLOCK = 1
PEER_LANE_SPLIT = 256
SEQ_CHUNK = 256
LRU_SCAN_CHANNELS = 256
MIB = 1024 * 1024


def _cparams(semantics, vmem_mib):
    return pltpu.CompilerParams(dimension_semantics=semantics, vmem_limit_bytes=vmem_mib * MIB)


def _pack_rows(x):
    return pltpu.bitcast(x.astype(BF16), jnp.uint32)


def _unpack_rows(w):
    return pltpu.bitcast(w, BF16)


def _dup_bf16_words(x):
    bits = pltpu.bitcast(x.astype(BF16).astype(F32), jnp.uint32)
    return bits | (bits >> 16)


def _dot(a, b):
    return jnp.dot(a, b, preferred_element_type=F32)


def _dot_nt(a, b):
    return lax.dot_general(a, b, (((1,), (1,)), ((), ())), preferred_element_type=F32)


def _split_bf16(x):
    hi = x.astype(BF16)
    lo = (x - hi.astype(F32)).astype(BF16)
    return hi, lo


def _dot3(x, w):
    xh, xl = _split_bf16(x)
    wh, wl = _split_bf16(w)
    return _dot(xh, wh) + _dot(xh, wl) + _dot(xl, wh)


def _rms_mod(x, g, shift, scale):
    ms = jnp.mean(x * x, axis=-1, keepdims=True)
    return (x * lax.rsqrt(ms + EPS) * g) * (1.0 + scale) + shift


def _mod_row_map(n_batch):
    return lambda b, j: (jnp.where(j == 0, n_batch, b), 0, 0)


def _ada_kernel(cv_ref, w_ref, b_ref, o_ref):
    cv = cv_ref[...]
    s = cv * jax.nn.sigmoid(cv)
    o_ref[0] = jnp.dot(s, w_ref[0], preferred_element_type=F32, precision=lax.Precision.HIGHEST) + b_ref[0]


def _ada_call(cvec, ada_w, ada_b):
    depth, d, d6 = ada_w.shape
    rows = cvec.shape[0]
    bn = 1024
    return pl.pallas_call(
        _ada_kernel,
        grid=(depth, d6 // bn),
        in_specs=[
            pl.BlockSpec((rows, d), lambda l, n: (0, 0)),
            pl.BlockSpec((1, d, bn), lambda l, n: (l, 0, n)),
            pl.BlockSpec((1, 1, bn), lambda l, n: (l, 0, n)),
        ],
        out_specs=pl.BlockSpec((1, rows, bn), lambda l, n: (l, 0, n)),
        out_shape=jax.ShapeDtypeStruct((depth, rows, d6), F32),
        compiler_params=_cparams(("arbitrary", "arbitrary"), 32),
        name="ada_mod",
    )(cvec, ada_w, ada_b.reshape(depth, 1, d6))


def _wk_kernel(wq_ref, k1_ref, k2_ref, o1_ref, o2_ref):
    w = wq_ref[0]
    half = N_KEYS
    hp = lax.Precision.HIGHEST
    dn = (((1,), (1,)), ((), ()))
    o1_ref[0] = lax.dot_general(k1_ref[0, 0], w[:, :half], dn, preferred_element_type=F32, precision=hp).astype(BF16)
    o2_ref[0] = lax.dot_general(k2_ref[0, 0], w[:, half:], dn, preferred_element_type=F32, precision=hp).astype(BF16)


def _wk_call(peer_wq, keys1, keys2):
    depth, d, _ = peer_wq.shape
    qd = 2 * N_KEYS
    out = jax.ShapeDtypeStruct((depth, PEER_HEADS * N_KEYS, d), BF16)
    return pl.pallas_call(
        _wk_kernel,
        grid=(depth, PEER_HEADS),
        in_specs=[
            pl.BlockSpec((1, d, qd), lambda l, h: (l, 0, h)),
            pl.BlockSpec((1, 1, N_KEYS, N_KEYS), lambda l, h: (l, h, 0, 0)),
            pl.BlockSpec((1, 1, N_KEYS, N_KEYS), lambda l, h: (l, h, 0, 0)),
        ],
        out_specs=[
            pl.BlockSpec((1, N_KEYS, d), lambda l, h: (l, h, 0)),
            pl.BlockSpec((1, N_KEYS, d), lambda l, h: (l, h, 0)),
        ],
        out_shape=[out, out],
        compiler_params=_cparams(("arbitrary", "arbitrary"), 32),
        name="peer_fold_keys",
    )(peer_wq, keys1, keys2)


def _attn_pre_kernel(x_ref, mod_ref, g_ref, w_ref, bd_ref, qg_ref, kg_ref, cos_ref, sm_ref, sp_ref,
                     q_ref, k_ref, v_ref):
    mod = mod_ref[0]
    h = _rms_mod(x_ref[0], g_ref[...], mod[0:1], mod[1:2]).astype(BF16)
    qkv = _dot(h, w_ref[...])
    nq = N_HEADS * HEAD_DIM
    nk = N_KV_HEADS * HEAD_DIM
    q, k, v = qkv[:, :nq], qkv[:, nq:nq + nk], qkv[:, nq + nk:]

    def head_norm(t, bd, gain):
        hi, lo = _split_bf16(t * t)
        ms = _dot(hi, bd) + _dot(lo, bd)
        return t * lax.rsqrt(ms + EPS) * gain

    def rope(t, width):
        reps = width // LANES
        cs = jnp.tile(cos_ref[...], (1, reps))
        sm = jnp.tile(sm_ref[...], (1, reps))
        sp = jnp.tile(sp_ref[...], (1, reps))
        quarter = HEAD_DIM // 4
        return t * cs + pltpu.roll(t, width - quarter, 1) * sm + pltpu.roll(t, quarter, 1) * sp

    qn = rope(head_norm(q, bd_ref[...], qg_ref[...]), nq)
    kn = rope(head_norm(k, bd_ref[0:nk, 0:nk], kg_ref[...]), nk)
    q_ref[0] = (qn * (HEAD_DIM ** -0.5)).astype(BF16)
    k_ref[0] = kn.astype(BF16)
    v_ref[0] = v.astype(BF16)


def _attn_kernel(n_latent, sink_ref, q_ref, kc_ref, vc_ref, kp_ref, kcur_ref, kn_ref, vp_ref, vcur_ref, vn_ref,
                 o_ref):
    j = pl.program_id(1)
    n_ctx = kc_ref.shape[1]
    blk = ATTN_BLOCK
    bi = j - n_ctx // blk
    q = q_ref[0]
    kcat = jnp.concatenate([kc_ref[0], kp_ref[0], kcur_ref[0], kn_ref[0]], axis=0)
    vcat = jnp.concatenate([vc_ref[0], vp_ref[0], vcur_ref[0], vn_ref[0]], axis=0)
    nk = n_ctx + 3 * blk
    rows = KV_GROUP * blk
    col = lax.broadcasted_iota(jnp.int32, (rows, nk), 1)
    qpos = bi * blk + lax.broadcasted_iota(jnp.int32, (rows, nk), 0) % blk
    kpos = (bi - 1) * blk + (col - n_ctx)
    in_window = (jnp.abs(qpos - kpos) <= WINDOW) & (kpos >= 0) & (kpos < n_latent) & (bi >= 0)
    valid = (col < n_ctx) | in_window
    ones = jnp.ones((nk, HEAD_DIM), BF16)
    for g in range(N_KV_HEADS):
        heads = range(g * KV_GROUP, (g + 1) * KV_GROUP)
        gs = slice(g * HEAD_DIM, (g + 1) * HEAD_DIM)
        qg = jnp.concatenate([q[:, h * HEAD_DIM:(h + 1) * HEAD_DIM] for h in heads], axis=0)
        sink = jnp.concatenate([jnp.full((blk, 1), sink_ref[h], F32) for h in heads], axis=0)
        s = jnp.where(valid, _dot_nt(qg, kcat[:, gs]), NEG_INF)
        m = jnp.maximum(jnp.max(s, axis=-1, keepdims=True), sink)
        p = jnp.exp(s - m).astype(BF16)
        oa = _dot(p, jnp.concatenate([vcat[:, gs], ones], axis=1))
        o = oa[:, :HEAD_DIM] / (oa[:, HEAD_DIM:HEAD_DIM + 1] + jnp.exp(sink - m))
        for i, h in enumerate(heads):
            o_ref[0, :, h * HEAD_DIM:(h + 1) * HEAD_DIM] = o[i * blk:(i + 1) * blk].astype(BF16)


def _rope_tables(n_ctx, n_latent):
    rows = n_latent // GRID_W
    row = jnp.repeat(jnp.arange(rows), GRID_W).astype(F32)
    col = jnp.tile(jnp.arange(GRID_W), rows).astype(F32)
    n_freq = HEAD_DIM // 4
    freqs = ROPE_THETA ** (-jnp.arange(n_freq, dtype=F32) / n_freq)
    ang = jnp.stack([row[:, None] * freqs, col[:, None] * freqs], axis=1)
    cos, sin = jnp.cos(ang), jnp.sin(ang)
    zero = jnp.zeros_like(sin)
    cs = jnp.stack([cos, cos], axis=2).reshape(n_latent, HEAD_DIM)
    sm = jnp.stack([-sin, zero], axis=2).reshape(n_latent, HEAD_DIM)
    sp = jnp.stack([zero, sin], axis=2).reshape(n_latent, HEAD_DIM)

    def full(tab, ctx_val):
        tab = jnp.concatenate([jnp.full((n_ctx, HEAD_DIM), ctx_val, F32), tab], axis=0)
        return jnp.tile(tab, (1, LANES // HEAD_DIM))

    return full(cs, 1.0), full(sm, 0.0), full(sp, 0.0)


def _attn_layer(x, mod, norm_g, w_qkv, q_gain, k_gain, sink, rope_tabs, n_ctx):
    n_batch, s, d = x.shape
    nt = s // TOKEN_TILE
    nq = N_HEADS * HEAD_DIM
    nk = N_KV_HEADS * HEAD_DIM
    eye = jnp.kron(jnp.eye(N_HEADS, dtype=F32), jnp.full((HEAD_DIM, HEAD_DIM), 1.0 / HEAD_DIM, F32)).astype(BF16)
    cs, sm, sp = rope_tabs
    tile_spec = lambda width: pl.BlockSpec((1, TOKEN_TILE, width), lambda b, j: (b, j, 0))
    const = lambda shape: pl.BlockSpec(shape, lambda b, j: (0,) * len(shape))
    tab_spec = pl.BlockSpec((TOKEN_TILE, LANES), lambda b, j: (j, 0))
    q, k, v = pl.pallas_call(
        _attn_pre_kernel,
        grid=(n_batch, nt),
        in_specs=[
            tile_spec(d),
            pl.BlockSpec((1, 6, d), _mod_row_map(n_batch)),
            const((1, d)),
            const((d, nq + 2 * nk)),
            const((nq, nq)),
            const((1, nq)),
            const((1, nk)),
            tab_spec, tab_spec, tab_spec,
        ],
        out_specs=[tile_spec(nq), tile_spec(nk), tile_spec(nk)],
        out_shape=[jax.ShapeDtypeStruct((n_batch, s, nq), BF16),
                   jax.ShapeDtypeStruct((n_batch, s, nk), BF16),
                   jax.ShapeDtypeStruct((n_batch, s, nk), BF16)],
        compiler_params=_cparams(("arbitrary", "arbitrary"), 48),
        name="attn_qkv",
    )(x, mod, norm_g.reshape(1, d), w_qkv.astype(BF16), eye,
      jnp.tile(q_gain, N_HEADS).reshape(1, nq), jnp.tile(k_gain, N_KV_HEADS).reshape(1, nk), cs, sm, sp)

    nb = s // ATTN_BLOCK
    blk_spec = lambda shift: pl.BlockSpec(
        (1, ATTN_BLOCK, nk), lambda b, j: (b, jnp.clip(j + shift, 0, nb - 1), 0))
    ctx_spec = pl.BlockSpec((1, n_ctx, nk), lambda b, j: (b, 0, 0))
    o = pl.pallas_call(
        functools.partial(_attn_kernel, s - n_ctx),
        grid=(n_batch, nb),
        in_specs=[
            pl.BlockSpec(memory_space=pltpu.SMEM),
            pl.BlockSpec((1, ATTN_BLOCK, nq), lambda b, j: (b, j, 0)),
            ctx_spec, ctx_spec,
            blk_spec(-1), blk_spec(0), blk_spec(1),
            blk_spec(-1), blk_spec(0), blk_spec(1),
        ],
        out_specs=pl.BlockSpec((1, ATTN_BLOCK, nq), lambda b, j: (b, j, 0)),
        out_shape=jax.ShapeDtypeStruct((n_batch, s, nq), BF16),
        compiler_params=_cparams(("arbitrary", "arbitrary"), 32),
        name="attn_core",
    )(sink, q, k, v, k, k, k, v, v, v)
    return o


def _lru_pre_kernel(x_ref, mod_ref, g_ref, w_ref, gate_ref, u_ref):
    mod = mod_ref[0]
    h = _rms_mod(x_ref[0], g_ref[...], mod[0:1], mod[1:2]).astype(BF16)
    y = _dot(h, w_ref[...])
    d = gate_ref.shape[-1]
    gate_ref[0] = jax.nn.gelu(y[:, :d]).astype(BF16)
    u_ref[0] = y[:, d:]


def _padded_copy(src_ref, pad_scr, n_ctx, pad):
    s = src_ref.shape[1]
    c = pad_scr.shape[1]
    zeros = jnp.zeros((pad, c), F32)
    pad_scr[0:pad] = zeros
    pad_scr[pad:pad + n_ctx] = src_ref[0, 0:n_ctx]
    pad_scr[pad + n_ctx:2 * pad + n_ctx] = zeros
    pad_scr[2 * pad + n_ctx:2 * pad + s] = src_ref[0, n_ctx:s]
    pad_scr[2 * pad + s:3 * pad + s] = zeros


def _padded_row(r, n_ctx, pad):
    return r + pad if r < n_ctx else r + 2 * pad


def _lru_seq_kernel(n_ctx, u_ref, gate_ref, cw_ref, cb_ref, wa_ref, ba_ref, wx_ref, bx_ref, lam_ref, y_ref,
                    pad_scr, a0_scr, b0_scr, a1_scr, b1_scr):
    s, c = u_ref.shape[1], u_ref.shape[2]
    blocks = [slice(i, i + LRU_BLOCK) for i in range(0, c, LRU_BLOCK)]
    pad = SUBLANES
    _padded_copy(u_ref, pad_scr, n_ctx, pad)
    a_scr = (a0_scr, a1_scr)
    b_scr = (b0_scr, b1_scr)
    cw = cw_ref[...]
    w_split = [[[_split_bf16(ref[d, n]) for n in range(len(blocks))] for ref in (wa_ref, wx_ref)]
               for d in range(2)]
    decay = [-LRU_C * jax.nn.softplus(-lam_ref[d:d + 1]) for d in range(2)]

    def gate_matmul(u_split, w):
        return jnp.concatenate(
            [_dot(xh, wh) + _dot(xh, wl) + _dot(xl, wh) for (xh, xl), (wh, wl) in zip(u_split, w)], axis=1)

    for ci in range(s // SEQ_CHUNK):
        r0 = ci * SEQ_CHUNK
        base = _padded_row(r0, n_ctx, pad) - LRU_CONV_LEFT
        u = cb_ref[...] + sum(pad_scr[base + k:base + k + SEQ_CHUNK] * cw[k:k + 1] for k in range(LRU_CONV))
        u_split = [_split_bf16(u[:, blk]) for blk in blocks]
        for d in range(2):
            r = jax.nn.sigmoid(gate_matmul(u_split, w_split[d][0]) + ba_ref[d:d + 1])
            i = jax.nn.sigmoid(gate_matmul(u_split, w_split[d][1]) + bx_ref[d:d + 1])
            log_a = r * decay[d]
            a = jnp.exp(log_a)
            a_scr[d][r0:r0 + SEQ_CHUNK] = a
            b_scr[d][r0:r0 + SEQ_CHUNK] = jnp.sqrt(jnp.tanh(-log_a) * (1.0 + a * a)) * (i * u)

    rid = lax.broadcasted_iota(jnp.int32, (SUBLANES, c), 0)
    n_steps = s // SUBLANES
    n_ctx_steps = n_ctx // SUBLANES

    def scan8(a, b, h_prev, reverse):
        for sh in (1, 2, 4):
            if reverse:
                keep = rid < SUBLANES - sh
                amt = SUBLANES - sh
            else:
                keep = rid >= sh
                amt = sh
            a_sh = jnp.where(keep, pltpu.roll(a, amt, 0), 1.0)
            b_sh = jnp.where(keep, pltpu.roll(b, amt, 0), 0.0)
            b = a * b_sh + b
            a = a * a_sh
        return b + a * h_prev

    def step(n, carry):
        hf, hb = carry
        rf = pl.multiple_of(n * SUBLANES, SUBLANES)
        nb = jnp.where(n < n_ctx_steps, n_ctx_steps - 1 - n, n_steps - 1 - n + n_ctx_steps)
        rb = pl.multiple_of(nb * SUBLANES, SUBLANES)
        out_f = scan8(a0_scr[pl.ds(rf, SUBLANES)], b0_scr[pl.ds(rf, SUBLANES)], hf, False)
        out_b = scan8(a1_scr[pl.ds(rb, SUBLANES)], b1_scr[pl.ds(rb, SUBLANES)], hb, True)
        b0_scr[pl.ds(rf, SUBLANES)] = out_f
        b1_scr[pl.ds(rb, SUBLANES)] = out_b
        return out_f[SUBLANES - 1:SUBLANES], out_b[0:1]

    zero = jnp.zeros((1, c), F32)
    lax.fori_loop(0, n_steps, step, (zero, zero))
    for ci in range(s // SEQ_CHUNK):
        rows = slice(ci * SEQ_CHUNK, (ci + 1) * SEQ_CHUNK)
        y_ref[0, rows] = ((b0_scr[rows] + b1_scr[rows]) * gate_ref[0, rows].astype(F32)).astype(BF16)


def _lru_layer(x, mod, norm_g, w_in, conv_w, conv_b, w_a, b_a, w_x, b_x, lam, n_ctx):
    n_batch, s, d = x.shape
    nt = s // TOKEN_TILE
    tile_spec = pl.BlockSpec((1, TOKEN_TILE, d), lambda b, j: (b, j, 0))
    gate, u = pl.pallas_call(
        _lru_pre_kernel,
        grid=(n_batch, nt),
        in_specs=[
            tile_spec,
            pl.BlockSpec((1, 6, d), _mod_row_map(n_batch)),
            pl.BlockSpec((1, d), lambda b, j: (0, 0)),
            pl.BlockSpec((d, 2 * d), lambda b, j: (0, 0)),
        ],
        out_specs=[tile_spec, tile_spec],
        out_shape=[jax.ShapeDtypeStruct((n_batch, s, d), BF16), jax.ShapeDtypeStruct((n_batch, s, d), F32)],
        compiler_params=_cparams(("arbitrary", "arbitrary"), 48),
        name="lru_in",
    )(x, mod, norm_g.reshape(1, d), w_in.astype(BF16))

    cblk = LRU_SCAN_CHANNELS
    ncb = d // cblk
    seq_spec = pl.BlockSpec((1, s, cblk), lambda b, c: (b, 0, c))
    vec2 = pl.BlockSpec((2, cblk), lambda b, c: (0, c))
    wspec = pl.BlockSpec((2, cblk // LRU_BLOCK, LRU_BLOCK, LRU_BLOCK), lambda b, c: (0, c, 0, 0))
    seq_scr = pltpu.VMEM((s, cblk), F32)
    return pl.pallas_call(
        functools.partial(_lru_seq_kernel, n_ctx),
        grid=(n_batch, ncb),
        in_specs=[
            seq_spec, seq_spec,
            pl.BlockSpec((LRU_CONV, cblk), lambda b, c: (0, c)),
            pl.BlockSpec((1, cblk), lambda b, c: (0, c)),
            wspec, vec2, wspec, vec2, vec2,
        ],
        out_specs=seq_spec,
        out_shape=jax.ShapeDtypeStruct((n_batch, s, d), BF16),
        scratch_shapes=[pltpu.VMEM((s + 3 * SUBLANES, cblk), F32), seq_scr, seq_scr, seq_scr, seq_scr],
        compiler_params=_cparams(("arbitrary", "arbitrary"), 48),
        name="lru_scan",
    )(u, gate, conv_w, conv_b.reshape(1, d), w_a, b_a, w_x, b_x, lam)


def _conf_pre_kernel(x_ref, mod_ref, g_ref, w_ref, b_ref, o_ref):
    mod = mod_ref[0]
    h = _rms_mod(x_ref[0], g_ref[...], mod[0:1], mod[1:2]).astype(BF16)
    y = _dot(h, w_ref[...]) + b_ref[...]
    d = o_ref.shape[-1]
    o_ref[0] = y[:, :d] * jax.nn.sigmoid(y[:, d:])


def _conf_conv_kernel(n_ctx, u_ref, w_ref, b_ref, o_ref, pad_scr):
    s = u_ref.shape[1]
    pad = 2 * SUBLANES
    left = CONF_KERNEL // 2
    _padded_copy(u_ref, pad_scr, n_ctx, pad)
    w = w_ref[...]
    for ci in range(s // SEQ_CHUNK):
        r0 = ci * SEQ_CHUNK
        base = _padded_row(r0, n_ctx, pad) - left
        acc = b_ref[...] + pad_scr[base:base + SEQ_CHUNK] * w[0:1]
        for k in range(1, CONF_KERNEL):
            acc = acc + pad_scr[base + k:base + k + SEQ_CHUNK] * w[k:k + 1]
        o_ref[0, r0:r0 + SEQ_CHUNK] = acc


def _conf_layer(x, mod, norm_g, w_pw1, b_pw1, dw_w, dw_b, n_ctx):
    n_batch, s, d = x.shape
    nt = s // TOKEN_TILE
    tile_spec = pl.BlockSpec((1, TOKEN_TILE, d), lambda b, j: (b, j, 0))
    glu = pl.pallas_call(
        _conf_pre_kernel,
        grid=(n_batch, nt),
        in_specs=[
            tile_spec,
            pl.BlockSpec((1, 6, d), _mod_row_map(n_batch)),
            pl.BlockSpec((1, d), lambda b, j: (0, 0)),
            pl.BlockSpec((d, 2 * d), lambda b, j: (0, 0)),
            pl.BlockSpec((1, 2 * d), lambda b, j: (0, 0)),
        ],
        out_specs=tile_spec,
        out_shape=jax.ShapeDtypeStruct((n_batch, s, d), F32),
        compiler_params=_cparams(("arbitrary", "arbitrary"), 48),
        name="conf_in",
    )(x, mod, norm_g.reshape(1, d), w_pw1.astype(BF16), b_pw1.reshape(1, 2 * d))

    cblk = LANES
    seq_spec = pl.BlockSpec((1, s, cblk), lambda b, c: (b, 0, c))
    return pl.pallas_call(
        functools.partial(_conf_conv_kernel, n_ctx),
        grid=(n_batch, d // cblk),
        in_specs=[
            seq_spec,
            pl.BlockSpec((CONF_KERNEL, cblk), lambda b, c: (0, c)),
            pl.BlockSpec((1, cblk), lambda b, c: (0, c)),
        ],
        out_specs=seq_spec,
        out_shape=jax.ShapeDtypeStruct((n_batch, s, d), F32),
        scratch_shapes=[pltpu.VMEM((s + 6 * SUBLANES, cblk), F32)],
        compiler_params=_cparams(("arbitrary", "arbitrary"), 32),
        name="conf_dwconv",
    )(glu, dw_w, dw_b.reshape(1, d))


def _compare_exchange(planes, i, j):
    hi = jnp.maximum(planes[i], planes[j])
    lo = jnp.minimum(planes[i], planes[j])
    planes[i], planes[j] = hi, lo


def _batcher_pairs(n):
    pairs = []

    def merge(lo, m, r):
        step = 2 * r
        if step < m:
            merge(lo, m, step)
            merge(lo + r, m, step)
            for i in range(lo + r, lo + m - r, step):
                pairs.append((i, i + r))
        else:
            pairs.append((lo, lo + r))

    def sort(lo, m):
        if m > 1:
            half = m // 2
            sort(lo, half)
            sort(lo + half, half)
            merge(lo, m, 1)

    sort(0, n)
    return pairs


_SORT16 = _batcher_pairs(PEER_TOPK)


def _sort_desc(planes):
    planes = list(planes)
    for i, j in _SORT16:
        _compare_exchange(planes, i, j)
    return planes


def _merge_top(a, b):
    n = len(a)
    planes = [jnp.maximum(a[i], b[n - 1 - i]) for i in range(n)]
    d = n // 2
    while d >= 1:
        for i in range(n):
            if i & d == 0:
                _compare_exchange(planes, i, i + d)
        d //= 2
    return planes


def _top_sorted(groups):
    groups = [_sort_desc(g) for g in groups]
    while len(groups) > 1:
        groups = [_merge_top(groups[i], groups[i + 1]) for i in range(0, len(groups), 2)]
    return groups[0]


def _count_prefix(pred, vals):
    assert len(vals) == 16
    c8 = pred(vals[7])
    c4 = pred(jnp.where(c8, vals[11], vals[3]))
    lo, hi = jnp.where(c8, vals[9], vals[1]), jnp.where(c8, vals[13], vals[5])
    c2 = pred(jnp.where(c4, hi, lo))
    even = [jnp.where(c8, vals[8 + i], vals[i]) for i in (0, 2, 4, 6)]
    lo, hi = jnp.where(c4, even[2], even[0]), jnp.where(c4, even[3], even[1])
    c1 = pred(jnp.where(c2, hi, lo))
    count = (jnp.where(c8, 8.0, 0.0) + jnp.where(c4, 4.0, 0.0)) + (jnp.where(c2, 2.0, 0.0) + jnp.where(c1, 1.0, 0.0))
    return jnp.where(pred(vals[15]), 16.0, count)


def _peer_stats(s1_scr, s2_scr, lane0, plane_scr, r2_ref, e2_ref, beta_ref, c1_ref):
    k = PEER_TOPK
    lanes = slice(lane0, lane0 + LANES)

    def top_planes(scr):
        groups = [[scr[(g * k + i) * PEER_HEADS:(g * k + i + 1) * PEER_HEADS, :] for i in range(k)]
                  for g in range(N_KEYS // k)]
        return _top_sorted(groups)

    v1 = top_planes(s1_scr)
    v2 = top_planes(s2_scr)
    cands = [v1[a] + v2[b] for a in range(k) for b in range(k) if (a + 1) * (b + 1) <= k]
    fill = jnp.full_like(v1[0], -jnp.inf)
    cands = cands + [fill] * (-len(cands) % k)
    top = _top_sorted([cands[i:i + k] for i in range(0, len(cands), k)])
    z = sum(jnp.exp(t - top[0]) for t in top)
    for b in range(k):
        plane_scr[b] = v2[b]
    plane_scr[k] = top[k - 1]
    plane_scr[k + 1] = v1[0]
    plane_scr[k + 2] = 0.5 / z
    part = N_KEYS // 2

    def head(h, carry):
        row = lambda i: plane_scr[i, pl.ds(h, 1), :]
        v2_h = [row(b) for b in range(k)]
        tau_h, top1_h, half_z_h = row(k), row(k + 1), row(k + 2)
        for p in range(0, N_KEYS, part):
            rows = pl.ds(p * PEER_HEADS + h, part, stride=PEER_HEADS)
            s1 = s1_scr[rows, :]
            beta = _count_prefix(lambda v: s1 + v >= tau_h, v2_h)
            s2 = s2_scr[rows, :]
            rank2 = _count_prefix(lambda v: v > s2, v2_h)
            beta_ref[h, p:p + part, lanes] = _dup_bf16_words(beta)
            r2_ref[h, p // 2:(p + part) // 2, lanes] = _pack_rows(rank2)
            e2_ref[h, p // 2:(p + part) // 2, lanes] = _pack_rows(jnp.exp(s2 - v2_h[0]))
            c1_ref[h, p:p + part, lanes] = _dup_bf16_words(jnp.exp(s1 - top1_h) * half_z_h)
        return carry

    lax.fori_loop(0, PEER_HEADS, head, 0)


def _post_kernel(ln_silu, yin_ref, x_ref, mod_ref, w_ref, b_ref, ng_ref, lng_ref, lnb_ref, wk1_ref, wk2_ref,
                 x1_ref, h2t_ref, r2_ref, e2_ref, beta_ref, c1_ref, s1_scr, s2_scr, plane_scr):
    mod = mod_ref[0]
    yin = yin_ref[0]
    if ln_silu:
        mu = jnp.mean(yin, axis=-1, keepdims=True)
        cen = yin - mu
        var = jnp.mean(cen * cen, axis=-1, keepdims=True)
        t = cen * lax.rsqrt(var + EPS) * lng_ref[...] + lnb_ref[...]
        yin = t * jax.nn.sigmoid(t)
    y = _dot(yin.astype(BF16), w_ref[...]) + b_ref[...]
    x1 = x_ref[0] + mod[2:3] * y
    x1_ref[...] = x1
    h2 = _rms_mod(x1, ng_ref[...], mod[3:4], mod[4:5])
    h2t = h2.T.astype(BF16)
    h2t_ref[...] = _pack_rows(h2t)
    s1 = _dot(wk1_ref[...], h2t)
    s2 = _dot(wk2_ref[...], h2t)
    for lc in range(s1_scr.shape[0]):
        s1_scr[lc] = s1[:, lc * LANES:(lc + 1) * LANES]
        s2_scr[lc] = s2[:, lc * LANES:(lc + 1) * LANES]
    for lc in range(s1_scr.shape[0]):
        _peer_stats(s1_scr.at[lc], s2_scr.at[lc], lc * LANES, plane_scr.at[lc], r2_ref, e2_ref, beta_ref, c1_ref)


def _post_call(ln_silu, yin, x, mod, w, bias, norm_g2, ln_g, ln_b, wk1_all, wk2_all, layer, latent_first):
    n_batch, s, d = x.shape
    nt = s // TOKEN_TILE
    n_tok = n_batch * s
    tile_spec = pl.BlockSpec((1, TOKEN_TILE, d), lambda b, j: (b, j, 0))
    const = lambda shape: pl.BlockSpec(shape, lambda b, j: (0,) * len(shape))

    def flat(b, j):
        if latent_first:
            return jnp.where(j == 0, n_batch * (nt - 1) + b, b * (nt - 1) + j - 1)
        return b * nt + j

    side_spec = pl.BlockSpec((PEER_HEADS, N_KEYS, TOKEN_TILE), lambda b, j: (0, 0, flat(b, j)))
    side_shape = jax.ShapeDtypeStruct((PEER_HEADS, N_KEYS, n_tok), jnp.uint32)
    pair_spec = pl.BlockSpec((PEER_HEADS, N_KEYS // 2, TOKEN_TILE), lambda b, j: (0, 0, flat(b, j)))
    pair_shape = jax.ShapeDtypeStruct((PEER_HEADS, N_KEYS // 2, n_tok), jnp.uint32)
    hk = PEER_HEADS * N_KEYS
    return pl.pallas_call(
        functools.partial(_post_kernel, ln_silu),
        grid=(n_batch, nt),
        in_specs=[
            tile_spec, tile_spec,
            pl.BlockSpec((1, 6, d), _mod_row_map(n_batch)),
            const((yin.shape[-1], d)), const((1, d)), const((1, d)), const((1, d)), const((1, d)),
            pl.BlockSpec((None, hk, d), lambda b, j: (layer, 0, 0)),
            pl.BlockSpec((None, hk, d), lambda b, j: (layer, 0, 0)),
        ],
        out_specs=[
            pl.BlockSpec((TOKEN_TILE, d), lambda b, j: (flat(b, j), 0)),
            pl.BlockSpec((d // 2, TOKEN_TILE), lambda b, j: (0, flat(b, j))),
            pair_spec, pair_spec, side_spec, side_spec,
        ],
        out_shape=[
            jax.ShapeDtypeStruct((n_tok, d), F32),
            jax.ShapeDtypeStruct((d // 2, n_tok), jnp.uint32),
            pair_shape, pair_shape, side_shape, side_shape,
        ],
        scratch_shapes=[pltpu.VMEM((TOKEN_TILE // LANES, hk, LANES), F32)] * 2
        + [pltpu.VMEM((TOKEN_TILE // LANES, PEER_TOPK + 3, PEER_HEADS, LANES), F32)],
        compiler_params=_cparams(("arbitrary", "arbitrary"), 56),
        name="mixer_out_peer_stats",
    )(yin, x, mod, w.astype(BF16), bias.reshape(1, d), norm_g2.reshape(1, d), ln_g.reshape(1, d),
      ln_b.reshape(1, d), wk1_all, wk2_all)


def _peer_gate_block(row0, lane0, act_ref, g_ref, r2_ref, e2_ref, beta_ref, c1_ref):
    et = act_ref.shape[0]
    pack = 2 * SUBLANES
    n_i1 = et // N_KEYS

    def bcast(ref, h, r, lanes):
        return _unpack_rows(jnp.broadcast_to(ref[h, r:r + 1, lanes], (SUBLANES, LANES)))[None]

    for cp in range(0, n_i1, PEER_I1_BLOCK):
        cs = tuple(range(cp, cp + PEER_I1_BLOCK))
        for lc in range(PEER_LANE_SPLIT // LANES):
            lanes = slice(lane0 + lc * LANES, lane0 + (lc + 1) * LANES)
            gates = [None for _ in cs]
            for h in range(PEER_HEADS):
                r2 = _unpack_rows(r2_ref[h, :, lanes]).reshape(N_KEYS // pack, pack, LANES)
                e2 = _unpack_rows(e2_ref[h, :, lanes]).reshape(N_KEYS // pack, pack, LANES)
                for n, c in enumerate(cs):
                    beta = bcast(beta_ref, h, row0 + c, lanes)
                    c1 = bcast(c1_ref, h, row0 + c, lanes)
                    term = jnp.where(r2 < beta, e2 * c1, jnp.zeros_like(e2))
                    gates[n] = term if h == 0 else gates[n] + term
            for n, c in enumerate(cs):
                rows = slice(c * N_KEYS, (c + 1) * N_KEYS)
                x = act_ref[rows, lanes]
                t = jnp.tanh(x * (GELU_C0 + GELU_C1 * (x * x)))
                g_ref[rows, lanes] = (gates[n].reshape(N_KEYS, LANES) * x) * (1.0 + t)


def _peer_kernel(n_mod, h2t_ref, r2_ref, e2_ref, bprev_ref, bcur_ref, cprev_ref, ccur_ref,
                 u0_ref, u1_ref, vt0_ref, vt1_ref, x1_ref, *rest):
    mod_refs, (o_ref, act0, act1, g0, g1, acc) = rest[:n_mod], rest[n_mod:]
    k = pl.program_id(1)
    et, tt = act0.shape
    half_rows = SUBLANES // 2
    odd = (half_rows, act1, g1, r2_ref, e2_ref, bprev_ref, cprev_ref)
    even = (0, act0, g0, r2_ref, e2_ref, bcur_ref, ccur_ref)

    def gate(stage, sl):
        row0, act, g, *side = stage
        _peer_gate_block(row0, sl.start, act, g, *side)

    @pl.when(k == 0)
    def _():
        act1[...] = jnp.zeros_like(act1)
        g0[...] = jnp.zeros_like(g0)
        g1[...] = jnp.zeros_like(g1)
        acc[...] = jnp.zeros_like(acc)

    splits = [slice(l, l + PEER_LANE_SPLIT) for l in range(0, tt, PEER_LANE_SPLIT)]
    last = pl.num_programs(1) - 1

    def stage_c(sl):
        acc[:, sl] += _dot(_unpack_rows(vt0_ref[...]), g0[:, sl]) + _dot(_unpack_rows(vt1_ref[...]), g1[:, sl])

    @pl.when(k < last)
    def _():
        for sl in splits:
            gate(odd, sl)
            act0[:, sl] = _dot(_unpack_rows(u0_ref[...]), _unpack_rows(h2t_ref[:, sl])).astype(BF16)
            act1[:, sl] = _dot(_unpack_rows(u1_ref[...]), _unpack_rows(h2t_ref[:, sl])).astype(BF16)
        for sl in splits:
            stage_c(sl)
            gate(even, sl)

    @pl.when(k == last)
    def _():
        for sl in splits:
            gate(odd, sl)
        for sl in splits:
            stage_c(sl)
        part = tt // n_mod
        for n, mod_ref in enumerate(mod_refs):
            rows = slice(n * part, (n + 1) * part)
            o_ref[rows] = x1_ref[rows] + mod_ref[0, 5:6] * acc[:, rows].T


def _peer_tables_kernel(u_ref, v_ref, up_ref, vtp_ref):
    up_ref[0] = _pack_rows(u_ref[0])
    vtp_ref[0] = _pack_rows(v_ref[0].T)


def _peer_tables_call(peer_u, peer_v):
    depth, n_exp, d = peer_u.shape
    rows = PEER_EXPERT_TILE
    tab_spec = pl.BlockSpec((1, rows, d), lambda l, e: (l, e, 0))
    return pl.pallas_call(
        _peer_tables_kernel,
        grid=(depth, n_exp // rows),
        in_specs=[tab_spec, tab_spec],
        out_specs=[pl.BlockSpec((1, rows // 2, d), lambda l, e: (l, e, 0)),
                   pl.BlockSpec((1, d // 2, rows), lambda l, e: (l, 0, e))],
        out_shape=[jax.ShapeDtypeStruct((depth, n_exp // 2, d), jnp.uint32),
                   jax.ShapeDtypeStruct((depth, d // 2, n_exp), jnp.uint32)],
        compiler_params=_cparams(("arbitrary", "arbitrary"), 32),
        name="peer_pack_tables",
    )(peer_u, peer_v)


def _peer_call(h2t, r2, e2, beta, c1, u_pk_all, vt_pk_all, layer, x1, mod, n_batch, latent_only):
    d, n_all = x1.shape[-1], h2t.shape[1]
    n_exp = vt_pk_all.shape[2]
    tt, et = PEER_TOKEN_TILE, PEER_EXPERT_TILE
    nt = (n_all // n_batch) // TOKEN_TILE
    n_tok = n_batch * (nt - 1) * TOKEN_TILE if latent_only else n_all
    assert n_tok % tt == 0
    n_pairs = n_exp // (2 * et)
    pair_spec = pl.BlockSpec((PEER_HEADS, N_KEYS // 2, tt), lambda i, k: (0, 0, i))
    n_mod = tt // TOKEN_TILE

    def mod_map(part):
        def index(i, k):
            t = i * n_mod + part
            if latent_only:
                return (t // (nt - 1), 0, 0)
            return (jnp.where(t % nt == 0, n_batch, t // nt), 0, 0)
        return index

    n_tiles = 2 * n_pairs

    def i1_spec(behind):
        return pl.BlockSpec((PEER_HEADS, SUBLANES, tt),
                            lambda i, k: (0, jnp.clip(k - behind, 0, n_pairs - 1), i))

    def u_spec(ahead):
        return pl.BlockSpec((None, et // 2, d), lambda i, k: (layer, jnp.minimum(2 * k + ahead, n_tiles - 1), 0))

    def vt_spec(behind):
        return pl.BlockSpec((None, d // 2, et), lambda i, k: (layer, 0, jnp.maximum(2 * k - behind, 0)))

    x1_spec = pl.BlockSpec((tt, d), lambda i, k: (jnp.where(k == 0, jnp.maximum(i - 1, 0), i), 0))
    assert et // N_KEYS == SUBLANES // 2
    return pl.pallas_call(
        functools.partial(_peer_kernel, n_mod),
        grid=(n_tok // tt, n_pairs + 1),
        in_specs=[
            pl.BlockSpec((d // 2, tt), lambda i, k: (0, i)),
            pair_spec, pair_spec,
            i1_spec(1), i1_spec(0), i1_spec(1), i1_spec(0),
            u_spec(0), u_spec(1), vt_spec(2), vt_spec(1),
            x1_spec,
        ] + [pl.BlockSpec((1, 6, d), mod_map(part)) for part in range(n_mod)],
        out_specs=pl.BlockSpec((tt, d), lambda i, k: (i, 0)),
        out_shape=jax.ShapeDtypeStruct((n_tok, d), F32),
        scratch_shapes=[pltpu.VMEM((et, tt), BF16)] * 4 + [pltpu.VMEM((d, tt), F32)],
        compiler_params=_cparams(("arbitrary", "arbitrary"), 56),
        name="peer_dense",
    )(h2t, r2, e2, beta, beta, c1, c1, u_pk_all, u_pk_all, vt_pk_all, vt_pk_all, x1,
      *([mod] * n_mod))


def kernel(x, c, ctx, c_ctx, ada_w, ada_b, norm_g, attn_wqkv, attn_q_gain, attn_k_gain, attn_sink, attn_wo,
           lru_w_in, lru_conv_w, lru_conv_b, lru_w_a, lru_b_a, lru_w_x, lru_b_x, lru_lambda, lru_w_out,
           conf_w_pw1, conf_b_pw1, conf_dw_w, conf_dw_b, conf_ln_g, conf_ln_b, conf_w_pw2, conf_b_pw2,
           peer_wq, peer_keys1, peer_keys2, peer_u, peer_v):
    n_batch, n_latent, d = x.shape
    n_ctx = ctx.shape[1]
    depth = ada_w.shape[0]
    assert n_ctx == TOKEN_TILE and n_latent % TOKEN_TILE == 0
    assert (n_batch * (n_ctx + n_latent)) % PEER_TOKEN_TILE == 0

    xs = jnp.concatenate([ctx, x], axis=1)
    mod_rows = -(-(n_batch + 1) // SUBLANES) * SUBLANES
    cvec = jnp.concatenate([c, c_ctx[None], jnp.zeros((mod_rows - n_batch - 1, d), F32)], axis=0)
    mods = _ada_call(cvec, ada_w, ada_b).reshape(depth, mod_rows, 6, d)
    wk1_all, wk2_all = _wk_call(peer_wq, peer_keys1, peer_keys2)
    key_major = lambda w: w.reshape(depth, PEER_HEADS, N_KEYS, d).transpose(0, 2, 1, 3).reshape(w.shape)
    wk1_all, wk2_all = key_major(wk1_all), key_major(wk2_all)
    u_pk_all, vt_pk_all = _peer_tables_call(peer_u, peer_v)
    rope_tabs = _rope_tables(n_ctx, n_latent)
    zeros_d = jnp.zeros((d,), F32)
    ones_d = jnp.ones((d,), F32)

    for layer in range(depth):
        kind, slot = layer % 3, layer // 3
        mod = mods[layer]
        if kind == 0:
            yin = _attn_layer(xs, mod, norm_g[layer, 0], attn_wqkv[slot], attn_q_gain[slot], attn_k_gain[slot],
                              attn_sink[slot], rope_tabs, n_ctx)
            post = (False, yin, xs, mod, attn_wo[slot], zeros_d, norm_g[layer, 1], ones_d, zeros_d)
        elif kind == 1:
            yin = _lru_layer(xs, mod, norm_g[layer, 0], lru_w_in[slot], lru_conv_w[slot], lru_conv_b[slot],
                             lru_w_a[slot], lru_b_a[slot], lru_w_x[slot], lru_b_x[slot], lru_lambda[slot], n_ctx)
            post = (False, yin, xs, mod, lru_w_out[slot], zeros_d, norm_g[layer, 1], ones_d, zeros_d)
        else:
            yin = _conf_layer(xs, mod, norm_g[layer, 0], conf_w_pw1[slot], conf_b_pw1[slot], conf_dw_w[slot],
                              conf_dw_b[slot], n_ctx)
            post = (True, yin, xs, mod, conf_w_pw2[slot], conf_b_pw2[slot], norm_g[layer, 1], conf_ln_g[slot],
                    conf_ln_b[slot])
        last = layer == depth - 1
        x1, h2t, r2, e2, beta, c1 = _post_call(*post, wk1_all, wk2_all, layer, last)
        x2 = _peer_call(h2t, r2, e2, beta, c1, u_pk_all, vt_pk_all, layer, x1, mod, n_batch, last)
        if not last:
            xs = x2.reshape(n_batch, n_ctx + n_latent, d)
    return x2.reshape(n_batch, n_latent, d)
```

```python
import functools

import jax
import jax.numpy as jnp
from jax import lax
from jax.experimental import pallas as pl
from jax.experimental.pallas import tpu as pltpu

F32 = jnp.float32
BF16 = jnp.bfloat16

EPS = 1e-6
NEG_INF = -1e30
GRID_W = 64
N_HEADS = 16
N_KV_HEADS = 4
HEAD_DIM = 64
KV_GROUP = N_HEADS // N_KV_HEADS
WINDOW = 128
ATTN_BLOCK = 128
ROPE_THETA = 10000.0
LRU_BLOCK = 128
LRU_CONV = 4
LRU_CONV_LEFT = 2
LRU_C = 8.0
CONF_KERNEL = 31
N_KEYS = 128
PEER_HEADS = 8
PEER_TOPK = 16
GELU_C0 = 0.7978845608028654
GELU_C1 = GELU_C0 * 0.044715

SUBLANES = 8
LANES = 128

TOKEN_TILE = 256
PEER_TOKEN_TILE = 1024
PEER_EXPERT_TILE = 512
PEER_I1_BLOCK = 1
PEER_LANE_SPLIT = 256
SEQ_CHUNK = 256
LRU_SCAN_CHANNELS = 256
MIB = 1024 * 1024


def _cparams(semantics, vmem_mib):
    return pltpu.CompilerParams(dimension_semantics=semantics, vmem_limit_bytes=vmem_mib * MIB)


def _pack_rows(x):
    return pltpu.bitcast(x.astype(BF16), jnp.uint32)


def _unpack_rows(w):
    return pltpu.bitcast(w, BF16)


def _dup_bf16_words(x):
    bits = pltpu.bitcast(x.astype(BF16).astype(F32), jnp.uint32)
    return bits | (bits >> 16)


def _dot(a, b):
    return jnp.dot(a, b, preferred_element_type=F32)


def _dot_nt(a, b):
    return lax.dot_general(a, b, (((1,), (1,)), ((), ())), preferred_element_type=F32)


def _split_bf16(x):
    hi = x.astype(BF16)
    lo = (x - hi.astype(F32)).astype(BF16)
    return hi, lo


def _dot3(x, w):
    xh, xl = _split_bf16(x)
    wh, wl = _split_bf16(w)
    return _dot(xh, wh) + _dot(xh, wl) + _dot(xl, wh)


def _rms_mod(x, g, shift, scale):
    ms = jnp.mean(x * x, axis=-1, keepdims=True)
    return (x * lax.rsqrt(ms + EPS) * g) * (1.0 + scale) + shift


def _mod_row_map(n_batch):
    return lambda b, j: (jnp.where(j == 0, n_batch, b), 0, 0)


def _ada_kernel(cv_ref, w_ref, b_ref, o_ref):
    cv = cv_ref[...]
    s = cv * jax.nn.sigmoid(cv)
    o_ref[0] = jnp.dot(s, w_ref[0], preferred_element_type=F32, precision=lax.Precision.HIGHEST) + b_ref[0]


def _ada_call(cvec, ada_w, ada_b):
    depth, d, d6 = ada_w.shape
    rows = cvec.shape[0]
    bn = 1024
    return pl.pallas_call(
        _ada_kernel,
        grid=(depth, d6 // bn),
        in_specs=[
            pl.BlockSpec((rows, d), lambda l, n: (0, 0)),
            pl.BlockSpec((1, d, bn), lambda l, n: (l, 0, n)),
            pl.BlockSpec((1, 1, bn), lambda l, n: (l, 0, n)),
        ],
        out_specs=pl.BlockSpec((1, rows, bn), lambda l, n: (l, 0, n)),
        out_shape=jax.ShapeDtypeStruct((depth, rows, d6), F32),
        compiler_params=_cparams(("arbitrary", "arbitrary"), 32),
        name="ada_mod",
    )(cvec, ada_w, ada_b.reshape(depth, 1, d6))


def _wk_kernel(wq_ref, k1_ref, k2_ref, o1_ref, o2_ref):
    w = wq_ref[0]
    half = N_KEYS
    hp = lax.Precision.HIGHEST
    dn = (((1,), (1,)), ((), ()))
    o1_ref[0] = lax.dot_general(k1_ref[0, 0], w[:, :half], dn, preferred_element_type=F32, precision=hp).astype(BF16)
    o2_ref[0] = lax.dot_general(k2_ref[0, 0], w[:, half:], dn, preferred_element_type=F32, precision=hp).astype(BF16)


def _wk_call(peer_wq, keys1, keys2):
    depth, d, _ = peer_wq.shape
    qd = 2 * N_KEYS
    out = jax.ShapeDtypeStruct((depth, PEER_HEADS * N_KEYS, d), BF16)
    return pl.pallas_call(
        _wk_kernel,
        grid=(depth, PEER_HEADS),
        in_specs=[
            pl.BlockSpec((1, d, qd), lambda l, h: (l, 0, h)),
            pl.BlockSpec((1, 1, N_KEYS, N_KEYS), lambda l, h: (l, h, 0, 0)),
            pl.BlockSpec((1, 1, N_KEYS, N_KEYS), lambda l, h: (l, h, 0, 0)),
        ],
        out_specs=[
            pl.BlockSpec((1, N_KEYS, d), lambda l, h: (l, h, 0)),
            pl.BlockSpec((1, N_KEYS, d), lambda l, h: (l, h, 0)),
        ],
        out_shape=[out, out],
        compiler_params=_cparams(("arbitrary", "arbitrary"), 32),
        name="peer_fold_keys",
    )(peer_wq, keys1, keys2)


def _attn_pre_kernel(x_ref, mod_ref, g_ref, w_ref, bd_ref, qg_ref, kg_ref, cos_ref, sm_ref, sp_ref,
                     q_ref, k_ref, v_ref):
    mod = mod_ref[0]
    h = _rms_mod(x_ref[0], g_ref[...], mod[0:1], mod[1:2]).astype(BF16)
    qkv = _dot(h, w_ref[...])
    nq = N_HEADS * HEAD_DIM
    nk = N_KV_HEADS * HEAD_DIM
    q, k, v = qkv[:, :nq], qkv[:, nq:nq + nk], qkv[:, nq + nk:]

    def head_norm(t, bd, gain):
        hi, lo = _split_bf16(t * t)
        ms = _dot(hi, bd) + _dot(lo, bd)
        return t * lax.rsqrt(ms + EPS) * gain

    def rope(t, width):
        reps = width // LANES
        cs = jnp.tile(cos_ref[...], (1, reps))
        sm = jnp.tile(sm_ref[...], (1, reps))
        sp = jnp.tile(sp_ref[...], (1, reps))
        quarter = HEAD_DIM // 4
        return t * cs + pltpu.roll(t, width - quarter, 1) * sm + pltpu.roll(t, quarter, 1) * sp

    qn = rope(head_norm(q, bd_ref[...], qg_ref[...]), nq)
    kn = rope(head_norm(k, bd_ref[0:nk, 0:nk], kg_ref[...]), nk)
    q_ref[0] = (qn * (HEAD_DIM ** -0.5)).astype(BF16)
    k_ref[0] = kn.astype(BF16)
    v_ref[0] = v.astype(BF16)


def _attn_kernel(n_latent, first_block, sink_ref, q_ref, kc_ref, vc_ref, kp_ref, kcur_ref, kn_ref, vp_ref,
                 vcur_ref, vn_ref, o_ref):
    j = pl.program_id(1) + first_block
    n_ctx = kc_ref.shape[1]
    blk = ATTN_BLOCK
    bi = j - n_ctx // blk
    q = q_ref[0]
    kcat = jnp.concatenate([kc_ref[0], kp_ref[0], kcur_ref[0], kn_ref[0]], axis=0)
    vcat = jnp.concatenate([vc_ref[0], vp_ref[0], vcur_ref[0], vn_ref[0]], axis=0)
    nk = n_ctx + 3 * blk
    rows = KV_GROUP * blk
    col = lax.broadcasted_iota(jnp.int32, (rows, nk), 1)
    qpos = bi * blk + lax.broadcasted_iota(jnp.int32, (rows, nk), 0) % blk
    kpos = (bi - 1) * blk + (col - n_ctx)
    in_window = (jnp.abs(qpos - kpos) <= WINDOW) & (kpos >= 0) & (kpos < n_latent) & (bi >= 0)
    valid = (col < n_ctx) | in_window
    ones = jnp.ones((nk, HEAD_DIM), BF16)
    for g in range(N_KV_HEADS):
        heads = range(g * KV_GROUP, (g + 1) * KV_GROUP)
        gs = slice(g * HEAD_DIM, (g + 1) * HEAD_DIM)
        qg = jnp.concatenate([q[:, h * HEAD_DIM:(h + 1) * HEAD_DIM] for h in heads], axis=0)
        sink = jnp.concatenate([jnp.full((blk, 1), sink_ref[h], F32) for h in heads], axis=0)
        s = jnp.where(valid, _dot_nt(qg, kcat[:, gs]), NEG_INF)
        m = jnp.maximum(jnp.max(s, axis=-1, keepdims=True), sink)
        p = jnp.exp(s - m).astype(BF16)
        oa = _dot(p, jnp.concatenate([vcat[:, gs], ones], axis=1))
        o = oa[:, :HEAD_DIM] / (oa[:, HEAD_DIM:HEAD_DIM + 1] + jnp.exp(sink - m))
        for i, h in enumerate(heads):
            o_ref[0, :, h * HEAD_DIM:(h + 1) * HEAD_DIM] = o[i * blk:(i + 1) * blk].astype(BF16)


def _rope_tables(n_ctx, n_latent):
    rows = n_latent // GRID_W
    row = jnp.repeat(jnp.arange(rows), GRID_W).astype(F32)
    col = jnp.tile(jnp.arange(GRID_W), rows).astype(F32)
    n_freq = HEAD_DIM // 4
    freqs = ROPE_THETA ** (-jnp.arange(n_freq, dtype=F32) / n_freq)
    ang = jnp.stack([row[:, None] * freqs, col[:, None] * freqs], axis=1)
    cos, sin = jnp.cos(ang), jnp.sin(ang)
    zero = jnp.zeros_like(sin)
    cs = jnp.stack([cos, cos], axis=2).reshape(n_latent, HEAD_DIM)
    sm = jnp.stack([-sin, zero], axis=2).reshape(n_latent, HEAD_DIM)
    sp = jnp.stack([zero, sin], axis=2).reshape(n_latent, HEAD_DIM)

    def full(tab, ctx_val):
        tab = jnp.concatenate([jnp.full((n_ctx, HEAD_DIM), ctx_val, F32), tab], axis=0)
        return jnp.tile(tab, (1, LANES // HEAD_DIM))

    return full(cs, 1.0), full(sm, 0.0), full(sp, 0.0)


def _attn_layer(x, mod, norm_g, w_qkv, q_gain, k_gain, sink, rope_tabs, n_ctx, need_ctx):
    n_batch, s, d = x.shape
    nt = s // TOKEN_TILE
    nq = N_HEADS * HEAD_DIM
    nk = N_KV_HEADS * HEAD_DIM
    eye = jnp.kron(jnp.eye(N_HEADS, dtype=F32), jnp.full((HEAD_DIM, HEAD_DIM), 1.0 / HEAD_DIM, F32)).astype(BF16)
    cs, sm, sp = rope_tabs
    tile_spec = lambda width: pl.BlockSpec((1, TOKEN_TILE, width), lambda b, j: (b, j, 0))
    const = lambda shape: pl.BlockSpec(shape, lambda b, j: (0,) * len(shape))
    tab_spec = pl.BlockSpec((TOKEN_TILE, LANES), lambda b, j: (j, 0))
    q, k, v = pl.pallas_call(
        _attn_pre_kernel,
        grid=(n_batch, nt),
        in_specs=[
            tile_spec(d),
            pl.BlockSpec((1, 6, d), _mod_row_map(n_batch)),
            const((1, d)),
            const((d, nq + 2 * nk)),
            const((nq, nq)),
            const((1, nq)),
            const((1, nk)),
            tab_spec, tab_spec, tab_spec,
        ],
        out_specs=[tile_spec(nq), tile_spec(nk), tile_spec(nk)],
        out_shape=[jax.ShapeDtypeStruct((n_batch, s, nq), BF16),
                   jax.ShapeDtypeStruct((n_batch, s, nk), BF16),
                   jax.ShapeDtypeStruct((n_batch, s, nk), BF16)],
        compiler_params=_cparams(("arbitrary", "arbitrary"), 48),
        name="attn_qkv",
    )(x, mod, norm_g.reshape(1, d), w_qkv.astype(BF16), eye,
      jnp.tile(q_gain, N_HEADS).reshape(1, nq), jnp.tile(k_gain, N_KV_HEADS).reshape(1, nk), cs, sm, sp)

    nb = s // ATTN_BLOCK
    off = 0 if need_ctx else n_ctx // ATTN_BLOCK
    blk_spec = lambda shift: pl.BlockSpec(
        (1, ATTN_BLOCK, nk), lambda b, j: (b, jnp.clip(j + off + shift, 0, nb - 1), 0))
    ctx_spec = pl.BlockSpec((1, n_ctx, nk), lambda b, j: (b, 0, 0))
    o = pl.pallas_call(
        functools.partial(_attn_kernel, s - n_ctx, off),
        grid=(n_batch, nb - off),
        in_specs=[
            pl.BlockSpec(memory_space=pltpu.SMEM),
            pl.BlockSpec((1, ATTN_BLOCK, nq), lambda b, j: (b, j + off, 0)),
            ctx_spec, ctx_spec,
            blk_spec(-1), blk_spec(0), blk_spec(1),
            blk_spec(-1), blk_spec(0), blk_spec(1),
        ],
        out_specs=pl.BlockSpec((1, ATTN_BLOCK, nq), lambda b, j: (b, j + off, 0)),
        out_shape=jax.ShapeDtypeStruct((n_batch, s, nq), BF16),
        compiler_params=_cparams(("arbitrary", "arbitrary"), 32),
        name="attn_core",
    )(sink, q, k, v, k, k, k, v, v, v)
    return o


def _lru_pre_kernel(x_ref, mod_ref, g_ref, w_ref, gate_ref, u_ref):
    mod = mod_ref[0]
    h = _rms_mod(x_ref[0], g_ref[...], mod[0:1], mod[1:2]).astype(BF16)
    y = _dot(h, w_ref[...])
    d = gate_ref.shape[-1]
    gate_ref[0] = jax.nn.gelu(y[:, :d]).astype(BF16)
    u_ref[0] = y[:, d:]


def _padded_copy(src_ref, pad_scr, n_ctx, pad):
    s = src_ref.shape[1]
    c = pad_scr.shape[1]
    zeros = jnp.zeros((pad, c), F32)
    pad_scr[0:pad] = zeros
    pad_scr[pad:pad + n_ctx] = src_ref[0, 0:n_ctx]
    pad_scr[pad + n_ctx:2 * pad + n_ctx] = zeros
    pad_scr[2 * pad + n_ctx:2 * pad + s] = src_ref[0, n_ctx:s]
    pad_scr[2 * pad + s:3 * pad + s] = zeros


def _padded_row(r, n_ctx, pad):
    return r + pad if r < n_ctx else r + 2 * pad


def _lru_seq_kernel(n_ctx, u_ref, gate_ref, cw_ref, cb_ref, wa_ref, ba_ref, wx_ref, bx_ref, lam_ref, y_ref,
                    pad_scr, a0_scr, b0_scr, a1_scr, b1_scr):
    s, c = u_ref.shape[1], u_ref.shape[2]
    blocks = [slice(i, i + LRU_BLOCK) for i in range(0, c, LRU_BLOCK)]
    pad = SUBLANES
    _padded_copy(u_ref, pad_scr, n_ctx, pad)
    a_scr = (a0_scr, a1_scr)
    b_scr = (b0_scr, b1_scr)
    cw = cw_ref[...]
    w_split = [[[_split_bf16(ref[d, n]) for n in range(len(blocks))] for ref in (wa_ref, wx_ref)]
               for d in range(2)]
    decay = [-LRU_C * jax.nn.softplus(-lam_ref[d:d + 1]) for d in range(2)]

    def gate_matmul(u_split, w):
        return jnp.concatenate(
            [_dot(xh, wh) + _dot(xh, wl) + _dot(xl, wh) for (xh, xl), (wh, wl) in zip(u_split, w)], axis=1)

    for ci in range(s // SEQ_CHUNK):
        r0 = ci * SEQ_CHUNK
        base = _padded_row(r0, n_ctx, pad) - LRU_CONV_LEFT
        u = cb_ref[...] + sum(pad_scr[base + k:base + k + SEQ_CHUNK] * cw[k:k + 1] for k in range(LRU_CONV))
        u_split = [_split_bf16(u[:, blk]) for blk in blocks]
        for d in range(2):
            r = jax.nn.sigmoid(gate_matmul(u_split, w_split[d][0]) + ba_ref[d:d + 1])
            i = jax.nn.sigmoid(gate_matmul(u_split, w_split[d][1]) + bx_ref[d:d + 1])
            log_a = r * decay[d]
            a = jnp.exp(log_a)
            a_scr[d][r0:r0 + SEQ_CHUNK] = a
            b_scr[d][r0:r0 + SEQ_CHUNK] = jnp.sqrt(jnp.tanh(-log_a) * (1.0 + a * a)) * (i * u)

    rid = lax.broadcasted_iota(jnp.int32, (SUBLANES, c), 0)
    n_steps = s // SUBLANES
    n_ctx_steps = n_ctx // SUBLANES

    def scan8(a, b, h_prev, reverse):
        for sh in (1, 2, 4):
            if reverse:
                keep = rid < SUBLANES - sh
                amt = SUBLANES - sh
            else:
                keep = rid >= sh
                amt = sh
            a_sh = jnp.where(keep, pltpu.roll(a, amt, 0), 1.0)
            b_sh = jnp.where(keep, pltpu.roll(b, amt, 0), 0.0)
            b = a * b_sh + b
            a = a * a_sh
        return b + a * h_prev

    def step(n, carry):
        hf, hb = carry
        rf = pl.multiple_of(n * SUBLANES, SUBLANES)
        nb = jnp.where(n < n_ctx_steps, n_ctx_steps - 1 - n, n_steps - 1 - n + n_ctx_steps)
        rb = pl.multiple_of(nb * SUBLANES, SUBLANES)
        out_f = scan8(a0_scr[pl.ds(rf, SUBLANES)], b0_scr[pl.ds(rf, SUBLANES)], hf, False)
        out_b = scan8(a1_scr[pl.ds(rb, SUBLANES)], b1_scr[pl.ds(rb, SUBLANES)], hb, True)
        b0_scr[pl.ds(rf, SUBLANES)] = out_f
        b1_scr[pl.ds(rb, SUBLANES)] = out_b
        return out_f[SUBLANES - 1:SUBLANES], out_b[0:1]

    zero = jnp.zeros((1, c), F32)
    lax.fori_loop(0, n_steps, step, (zero, zero))
    for ci in range(s // SEQ_CHUNK):
        rows = slice(ci * SEQ_CHUNK, (ci + 1) * SEQ_CHUNK)
        y_ref[0, rows] = ((b0_scr[rows] + b1_scr[rows]) * gate_ref[0, rows].astype(F32)).astype(BF16)


def _lru_layer(x, mod, norm_g, w_in, conv_w, conv_b, w_a, b_a, w_x, b_x, lam, n_ctx):
    n_batch, s, d = x.shape
    nt = s // TOKEN_TILE
    tile_spec = pl.BlockSpec((1, TOKEN_TILE, d), lambda b, j: (b, j, 0))
    gate, u = pl.pallas_call(
        _lru_pre_kernel,
        grid=(n_batch, nt),
        in_specs=[
            tile_spec,
            pl.BlockSpec((1, 6, d), _mod_row_map(n_batch)),
            pl.BlockSpec((1, d), lambda b, j: (0, 0)),
            pl.BlockSpec((d, 2 * d), lambda b, j: (0, 0)),
        ],
        out_specs=[tile_spec, tile_spec],
        out_shape=[jax.ShapeDtypeStruct((n_batch, s, d), BF16), jax.ShapeDtypeStruct((n_batch, s, d), F32)],
        compiler_params=_cparams(("arbitrary", "arbitrary"), 48),
        name="lru_in",
    )(x, mod, norm_g.reshape(1, d), w_in.astype(BF16))

    cblk = LRU_SCAN_CHANNELS
    ncb = d // cblk
    seq_spec = pl.BlockSpec((1, s, cblk), lambda b, c: (b, 0, c))
    vec2 = pl.BlockSpec((2, cblk), lambda b, c: (0, c))
    wspec = pl.BlockSpec((2, cblk // LRU_BLOCK, LRU_BLOCK, LRU_BLOCK), lambda b, c: (0, c, 0, 0))
    seq_scr = pltpu.VMEM((s, cblk), F32)
    return pl.pallas_call(
        functools.partial(_lru_seq_kernel, n_ctx),
        grid=(n_batch, ncb),
        in_specs=[
            seq_spec, seq_spec,
            pl.BlockSpec((LRU_CONV, cblk), lambda b, c: (0, c)),
            pl.BlockSpec((1, cblk), lambda b, c: (0, c)),
            wspec, vec2, wspec, vec2, vec2,
        ],
        out_specs=seq_spec,
        out_shape=jax.ShapeDtypeStruct((n_batch, s, d), BF16),
        scratch_shapes=[pltpu.VMEM((s + 3 * SUBLANES, cblk), F32), seq_scr, seq_scr, seq_scr, seq_scr],
        compiler_params=_cparams(("arbitrary", "arbitrary"), 48),
        name="lru_scan",
    )(u, gate, conv_w, conv_b.reshape(1, d), w_a, b_a, w_x, b_x, lam)


def _conf_pre_kernel(x_ref, mod_ref, g_ref, w_ref, b_ref, o_ref):
    mod = mod_ref[0]
    h = _rms_mod(x_ref[0], g_ref[...], mod[0:1], mod[1:2]).astype(BF16)
    y = _dot(h, w_ref[...]) + b_ref[...]
    d = o_ref.shape[-1]
    o_ref[0] = y[:, :d] * jax.nn.sigmoid(y[:, d:])


def _conf_conv_kernel(n_ctx, u_ref, w_ref, b_ref, o_ref, pad_scr):
    s = u_ref.shape[1]
    pad = 2 * SUBLANES
    left = CONF_KERNEL // 2
    _padded_copy(u_ref, pad_scr, n_ctx, pad)
    w = w_ref[...]
    for ci in range(s // SEQ_CHUNK):
        r0 = ci * SEQ_CHUNK
        base = _padded_row(r0, n_ctx, pad) - left
        acc = b_ref[...] + pad_scr[base:base + SEQ_CHUNK] * w[0:1]
        for k in range(1, CONF_KERNEL):
            acc = acc + pad_scr[base + k:base + k + SEQ_CHUNK] * w[k:k + 1]
        o_ref[0, r0:r0 + SEQ_CHUNK] = acc


def _conf_layer(x, mod, norm_g, w_pw1, b_pw1, dw_w, dw_b, n_ctx):
    n_batch, s, d = x.shape
    nt = s // TOKEN_TILE
    tile_spec = pl.BlockSpec((1, TOKEN_TILE, d), lambda b, j: (b, j, 0))
    glu = pl.pallas_call(
        _conf_pre_kernel,
        grid=(n_batch, nt),
        in_specs=[
            tile_spec,
            pl.BlockSpec((1, 6, d), _mod_row_map(n_batch)),
            pl.BlockSpec((1, d), lambda b, j: (0, 0)),
            pl.BlockSpec((d, 2 * d), lambda b, j: (0, 0)),
            pl.BlockSpec((1, 2 * d), lambda b, j: (0, 0)),
        ],
        out_specs=tile_spec,
        out_shape=jax.ShapeDtypeStruct((n_batch, s, d), F32),
        compiler_params=_cparams(("arbitrary", "arbitrary"), 48),
        name="conf_in",
    )(x, mod, norm_g.reshape(1, d), w_pw1.astype(BF16), b_pw1.reshape(1, 2 * d))

    cblk = LANES
    seq_spec = pl.BlockSpec((1, s, cblk), lambda b, c: (b, 0, c))
    return pl.pallas_call(
        functools.partial(_conf_conv_kernel, n_ctx),
        grid=(n_batch, d // cblk),
        in_specs=[
            seq_spec,
            pl.BlockSpec((CONF_KERNEL, cblk), lambda b, c: (0, c)),
            pl.BlockSpec((1, cblk), lambda b, c: (0, c)),
        ],
        out_specs=seq_spec,
        out_shape=jax.ShapeDtypeStruct((n_batch, s, d), F32),
        scratch_shapes=[pltpu.VMEM((s + 6 * SUBLANES, cblk), F32)],
        compiler_params=_cparams(("arbitrary", "arbitrary"), 32),
        name="conf_dwconv",
    )(glu, dw_w, dw_b.reshape(1, d))


def _compare_exchange(planes, i, j):
    hi = jnp.maximum(planes[i], planes[j])
    lo = jnp.minimum(planes[i], planes[j])
    planes[i], planes[j] = hi, lo


def _batcher_pairs(n):
    pairs = []

    def merge(lo, m, r):
        step = 2 * r
        if step < m:
            merge(lo, m, step)
            merge(lo + r, m, step)
            for i in range(lo + r, lo + m - r, step):
                pairs.append((i, i + r))
        else:
            pairs.append((lo, lo + r))

    def sort(lo, m):
        if m > 1:
            half = m // 2
            sort(lo, half)
            sort(lo + half, half)
            merge(lo, m, 1)

    sort(0, n)
    return pairs


_SORT16 = _batcher_pairs(PEER_TOPK)


def _sort_desc(planes):
    planes = list(planes)
    for i, j in _SORT16:
        _compare_exchange(planes, i, j)
    return planes


def _merge_top(a, b):
    n = len(a)
    planes = [jnp.maximum(a[i], b[n - 1 - i]) for i in range(n)]
    d = n // 2
    while d >= 1:
        for i in range(n):
            if i & d == 0:
                _compare_exchange(planes, i, i + d)
        d //= 2
    return planes


def _top_sorted(groups):
    groups = [_sort_desc(g) for g in groups]
    while len(groups) > 1:
        groups = [_merge_top(groups[i], groups[i + 1]) for i in range(0, len(groups), 2)]
    return groups[0]


def _count_prefix(pred, vals):
    assert len(vals) == 16
    c8 = pred(vals[7])
    c4 = pred(jnp.where(c8, vals[11], vals[3]))
    lo, hi = jnp.where(c8, vals[9], vals[1]), jnp.where(c8, vals[13], vals[5])
    c2 = pred(jnp.where(c4, hi, lo))
    even = [jnp.where(c8, vals[8 + i], vals[i]) for i in (0, 2, 4, 6)]
    lo, hi = jnp.where(c4, even[2], even[0]), jnp.where(c4, even[3], even[1])
    c1 = pred(jnp.where(c2, hi, lo))
    count = (jnp.where(c8, 8.0, 0.0) + jnp.where(c4, 4.0, 0.0)) + (jnp.where(c2, 2.0, 0.0) + jnp.where(c1, 1.0, 0.0))
    return jnp.where(pred(vals[15]), 16.0, count)


def _peer_stats(s1_scr, s2_scr, lane0, plane_scr, r2_ref, e2_ref, beta_ref, c1_ref):
    k = PEER_TOPK
    lanes = slice(lane0, lane0 + LANES)

    def top_planes(scr):
        groups = [[scr[(g * k + i) * PEER_HEADS:(g * k + i + 1) * PEER_HEADS, :] for i in range(k)]
                  for g in range(N_KEYS // k)]
        return _top_sorted(groups)

    v1 = top_planes(s1_scr)
    v2 = top_planes(s2_scr)
    cands = [v1[a] + v2[b] for a in range(k) for b in range(k) if (a + 1) * (b + 1) <= k]
    fill = jnp.full_like(v1[0], -jnp.inf)
    cands = cands + [fill] * (-len(cands) % k)
    top = _top_sorted([cands[i:i + k] for i in range(0, len(cands), k)])
    z = sum(jnp.exp(t - top[0]) for t in top)
    for b in range(k):
        plane_scr[b] = v2[b]
    plane_scr[k] = top[k - 1]
    plane_scr[k + 1] = v1[0]
    plane_scr[k + 2] = 0.5 / z
    part = N_KEYS // 2

    def head(h, carry):
        row = lambda i: plane_scr[i, pl.ds(h, 1), :]
        v2_h = [row(b) for b in range(k)]
        tau_h, top1_h, half_z_h = row(k), row(k + 1), row(k + 2)
        for p in range(0, N_KEYS, part):
            rows = pl.ds(p * PEER_HEADS + h, part, stride=PEER_HEADS)
            s1 = s1_scr[rows, :]
            beta = _count_prefix(lambda v: s1 + v >= tau_h, v2_h)
            s2 = s2_scr[rows, :]
            rank2 = _count_prefix(lambda v: v > s2, v2_h)
            beta_ref[h, p:p + part, lanes] = _dup_bf16_words(beta)
            r2_ref[h, p // 2:(p + part) // 2, lanes] = _pack_rows(rank2)
            e2_ref[h, p // 2:(p + part) // 2, lanes] = _pack_rows(jnp.exp(s2 - v2_h[0]))
            c1_ref[h, p:p + part, lanes] = _dup_bf16_words(jnp.exp(s1 - top1_h) * half_z_h)
        return carry

    lax.fori_loop(0, PEER_HEADS, head, 0)


def _post_kernel(ln_silu, yin_ref, x_ref, mod_ref, w_ref, b_ref, ng_ref, lng_ref, lnb_ref, wk1_ref, wk2_ref,
                 x1_ref, h2t_ref, r2_ref, e2_ref, beta_ref, c1_ref, s1_scr, s2_scr, plane_scr):
    mod = mod_ref[0]
    yin = yin_ref[0]
    if ln_silu:
        mu = jnp.mean(yin, axis=-1, keepdims=True)
        cen = yin - mu
        var = jnp.mean(cen * cen, axis=-1, keepdims=True)
        t = cen * lax.rsqrt(var + EPS) * lng_ref[...] + lnb_ref[...]
        yin = t * jax.nn.sigmoid(t)
    y = _dot(yin.astype(BF16), w_ref[...]) + b_ref[...]
    x1 = x_ref[0] + mod[2:3] * y
    x1_ref[...] = x1
    h2 = _rms_mod(x1, ng_ref[...], mod[3:4], mod[4:5])
    h2t = h2.T.astype(BF16)
    h2t_ref[...] = _pack_rows(h2t)
    s1 = _dot(wk1_ref[...], h2t)
    s2 = _dot(wk2_ref[...], h2t)
    for lc in range(s1_scr.shape[0]):
        s1_scr[lc] = s1[:, lc * LANES:(lc + 1) * LANES]
        s2_scr[lc] = s2[:, lc * LANES:(lc + 1) * LANES]
    for lc in range(s1_scr.shape[0]):
        _peer_stats(s1_scr.at[lc], s2_scr.at[lc], lc * LANES, plane_scr.at[lc], r2_ref, e2_ref, beta_ref, c1_ref)


def _post_call(ln_silu, yin, x, mod, w, bias, norm_g2, ln_g, ln_b, wk1_all, wk2_all, layer, latent_first):
    n_batch, s, d = x.shape
    nt = s // TOKEN_TILE
    n_tok = n_batch * s
    j0 = 1 if latent_first else 0
    tile_spec = pl.BlockSpec((1, TOKEN_TILE, d), lambda b, j: (b, j + j0, 0))
    const = lambda shape: pl.BlockSpec(shape, lambda b, j: (0,) * len(shape))
    mod_map = (lambda b, j: (b, 0, 0)) if latent_first else _mod_row_map(n_batch)

    def flat(b, j):
        return b * (nt - j0) + j

    side_spec = pl.BlockSpec((PEER_HEADS, N_KEYS, TOKEN_TILE), lambda b, j: (0, 0, flat(b, j)))
    side_shape = jax.ShapeDtypeStruct((PEER_HEADS, N_KEYS, n_tok), jnp.uint32)
    pair_spec = pl.BlockSpec((PEER_HEADS, N_KEYS // 2, TOKEN_TILE), lambda b, j: (0, 0, flat(b, j)))
    pair_shape = jax.ShapeDtypeStruct((PEER_HEADS, N_KEYS // 2, n_tok), jnp.uint32)
    hk = PEER_HEADS * N_KEYS
    return pl.pallas_call(
        functools.partial(_post_kernel, ln_silu),
        grid=(n_batch, nt - j0),
        in_specs=[
            tile_spec, tile_spec,
            pl.BlockSpec((1, 6, d), mod_map),
            const((yin.shape[-1], d)), const((1, d)), const((1, d)), const((1, d)), const((1, d)),
            pl.BlockSpec((None, hk, d), lambda b, j: (layer, 0, 0)),
            pl.BlockSpec((None, hk, d), lambda b, j: (layer, 0, 0)),
        ],
        out_specs=[
            pl.BlockSpec((TOKEN_TILE, d), lambda b, j: (flat(b, j), 0)),
            pl.BlockSpec((d // 2, TOKEN_TILE), lambda b, j: (0, flat(b, j))),
            pair_spec, pair_spec, side_spec, side_spec,
        ],
        out_shape=[
            jax.ShapeDtypeStruct((n_tok, d), F32),
            jax.ShapeDtypeStruct((d // 2, n_tok), jnp.uint32),
            pair_shape, pair_shape, side_shape, side_shape,
        ],
        scratch_shapes=[pltpu.VMEM((TOKEN_TILE // LANES, hk, LANES), F32)] * 2
        + [pltpu.VMEM((TOKEN_TILE // LANES, PEER_TOPK + 3, PEER_HEADS, LANES), F32)],
        compiler_params=_cparams(("arbitrary", "arbitrary"), 56),
        name="mixer_out_peer_stats",
    )(yin, x, mod, w.astype(BF16), bias.reshape(1, d), norm_g2.reshape(1, d), ln_g.reshape(1, d),
      ln_b.reshape(1, d), wk1_all, wk2_all)


def _peer_gate_block(row0, lane0, act_ref, g_ref, r2_ref, e2_ref, beta_ref, c1_ref):
    et = act_ref.shape[0]
    pack = 2 * SUBLANES
    n_i1 = et // N_KEYS

    def bcast(ref, h, r, lanes):
        return _unpack_rows(jnp.broadcast_to(ref[h, r:r + 1, lanes], (SUBLANES, LANES)))[None]

    for cp in range(0, n_i1, PEER_I1_BLOCK):
        cs = tuple(range(cp, cp + PEER_I1_BLOCK))
        for lc in range(PEER_LANE_SPLIT // LANES):
            lanes = slice(lane0 + lc * LANES, lane0 + (lc + 1) * LANES)
            gates = [None for _ in cs]
            for h in range(PEER_HEADS):
                r2 = _unpack_rows(r2_ref[h, :, lanes]).reshape(N_KEYS // pack, pack, LANES)
                e2 = _unpack_rows(e2_ref[h, :, lanes]).reshape(N_KEYS // pack, pack, LANES)
                for n, c in enumerate(cs):
                    beta = bcast(beta_ref, h, row0 + c, lanes)
                    c1 = bcast(c1_ref, h, row0 + c, lanes)
                    term = jnp.where(r2 < beta, e2 * c1, jnp.zeros_like(e2))
                    gates[n] = term if h == 0 else gates[n] + term
            for n, c in enumerate(cs):
                rows = slice(c * N_KEYS, (c + 1) * N_KEYS)
                x = act_ref[rows, lanes]
                t = jnp.tanh(x * (GELU_C0 + GELU_C1 * (x * x)))
                g_ref[rows, lanes] = (gates[n].reshape(N_KEYS, LANES) * x) * (1.0 + t)


def _peer_kernel(n_mod, h2t_ref, r2_ref, e2_ref, bprev_ref, bcur_ref, cprev_ref, ccur_ref,
                 u0_ref, u1_ref, vt0_ref, vt1_ref, x1_ref, *rest):
    mod_refs, (o_ref, act0, act1, g0, g1, acc) = rest[:n_mod], rest[n_mod:]
    k = pl.program_id(1)
    et, tt = act0.shape
    half_rows = SUBLANES // 2
    odd = (half_rows, act1, g1, r2_ref, e2_ref, bprev_ref, cprev_ref)
    even = (0, act0, g0, r2_ref, e2_ref, bcur_ref, ccur_ref)

    def gate(stage, sl):
        row0, act, g, *side = stage
        _peer_gate_block(row0, sl.start, act, g, *side)

    @pl.when(k == 0)
    def _():
        act1[...] = jnp.zeros_like(act1)
        g0[...] = jnp.zeros_like(g0)
        g1[...] = jnp.zeros_like(g1)
        acc[...] = jnp.zeros_like(acc)

    splits = [slice(l, l + PEER_LANE_SPLIT) for l in range(0, tt, PEER_LANE_SPLIT)]
    last = pl.num_programs(1) - 1

    def stage_c(sl):
        acc[:, sl] += _dot(_unpack_rows(vt0_ref[...]), g0[:, sl]) + _dot(_unpack_rows(vt1_ref[...]), g1[:, sl])

    @pl.when(k < last)
    def _():
        for sl in splits:
            gate(odd, sl)
            act0[:, sl] = _dot(_unpack_rows(u0_ref[...]), _unpack_rows(h2t_ref[:, sl])).astype(BF16)
            act1[:, sl] = _dot(_unpack_rows(u1_ref[...]), _unpack_rows(h2t_ref[:, sl])).astype(BF16)
        for sl in splits:
            stage_c(sl)
            gate(even, sl)

    @pl.when(k == last)
    def _():
        for sl in splits:
            gate(odd, sl)
        for sl in splits:
            stage_c(sl)
        part = tt // n_mod
        for n, mod_ref in enumerate(mod_refs):
            rows = slice(n * part, (n + 1) * part)
            o_ref[rows] = x1_ref[rows] + mod_ref[0, 5:6] * acc[:, rows].T


def _peer_tables_kernel(u_ref, v_ref, up_ref, vtp_ref):
    up_ref[0] = _pack_rows(u_ref[0])
    vtp_ref[0] = _pack_rows(v_ref[0].T)


def _peer_tables_call(peer_u, peer_v):
    depth, n_exp, d = peer_u.shape
    rows = PEER_EXPERT_TILE
    tab_spec = pl.BlockSpec((1, rows, d), lambda l, e: (l, e, 0))
    return pl.pallas_call(
        _peer_tables_kernel,
        grid=(depth, n_exp // rows),
        in_specs=[tab_spec, tab_spec],
        out_specs=[pl.BlockSpec((1, rows // 2, d), lambda l, e: (l, e, 0)),
                   pl.BlockSpec((1, d // 2, rows), lambda l, e: (l, 0, e))],
        out_shape=[jax.ShapeDtypeStruct((depth, n_exp // 2, d), jnp.uint32),
                   jax.ShapeDtypeStruct((depth, d // 2, n_exp), jnp.uint32)],
        compiler_params=_cparams(("arbitrary", "arbitrary"), 32),
        name="peer_pack_tables",
    )(peer_u, peer_v)


def _peer_call(h2t, r2, e2, beta, c1, u_pk_all, vt_pk_all, layer, x1, mod, n_batch, latent_only):
    d, n_all = x1.shape[-1], h2t.shape[1]
    n_exp = vt_pk_all.shape[2]
    tt, et = PEER_TOKEN_TILE, PEER_EXPERT_TILE
    nt = (n_all // n_batch) // TOKEN_TILE
    n_tok = n_batch * (nt - 1) * TOKEN_TILE if latent_only else n_all
    assert n_tok % tt == 0
    n_pairs = n_exp // (2 * et)
    pair_spec = pl.BlockSpec((PEER_HEADS, N_KEYS // 2, tt), lambda i, k: (0, 0, i))
    n_mod = tt // TOKEN_TILE

    def mod_map(part):
        def index(i, k):
            t = i * n_mod + part
            if latent_only:
                return (t // (nt - 1), 0, 0)
            return (jnp.where(t % nt == 0, n_batch, t // nt), 0, 0)
        return index

    n_tiles = 2 * n_pairs

    def i1_spec(behind):
        return pl.BlockSpec((PEER_HEADS, SUBLANES, tt),
                            lambda i, k: (0, jnp.clip(k - behind, 0, n_pairs - 1), i))

    def u_spec(ahead):
        return pl.BlockSpec((None, et // 2, d), lambda i, k: (layer, jnp.minimum(2 * k + ahead, n_tiles - 1), 0))

    def vt_spec(behind):
        return pl.BlockSpec((None, d // 2, et), lambda i, k: (layer, 0, jnp.maximum(2 * k - behind, 0)))

    x1_spec = pl.BlockSpec((tt, d), lambda i, k: (jnp.where(k == 0, jnp.maximum(i - 1, 0), i), 0))
    assert et // N_KEYS == SUBLANES // 2
    return pl.pallas_call(
        functools.partial(_peer_kernel, n_mod),
        grid=(n_tok // tt, n_pairs + 1),
        in_specs=[
            pl.BlockSpec((d // 2, tt), lambda i, k: (0, i)),
            pair_spec, pair_spec,
            i1_spec(1), i1_spec(0), i1_spec(1), i1_spec(0),
            u_spec(0), u_spec(1), vt_spec(2), vt_spec(1),
            x1_spec,
        ] + [pl.BlockSpec((1, 6, d), mod_map(part)) for part in range(n_mod)],
        out_specs=pl.BlockSpec((tt, d), lambda i, k: (i, 0)),
        out_shape=jax.ShapeDtypeStruct((n_tok, d), F32),
        scratch_shapes=[pltpu.VMEM((et, tt), BF16)] * 4 + [pltpu.VMEM((d, tt), F32)],
        compiler_params=_cparams(("arbitrary", "arbitrary"), 56),
        name="peer_dense",
    )(h2t, r2, e2, beta, beta, c1, c1, u_pk_all, u_pk_all, vt_pk_all, vt_pk_all, x1,
      *([mod] * n_mod))


def kernel(x, c, ctx, c_ctx, ada_w, ada_b, norm_g, attn_wqkv, attn_q_gain, attn_k_gain, attn_sink, attn_wo,
           lru_w_in, lru_conv_w, lru_conv_b, lru_w_a, lru_b_a, lru_w_x, lru_b_x, lru_lambda, lru_w_out,
           conf_w_pw1, conf_b_pw1, conf_dw_w, conf_dw_b, conf_ln_g, conf_ln_b, conf_w_pw2, conf_b_pw2,
           peer_wq, peer_keys1, peer_keys2, peer_u, peer_v):
    n_batch, n_latent, d = x.shape
    n_ctx = ctx.shape[1]
    depth = ada_w.shape[0]
    assert n_ctx == TOKEN_TILE and n_latent % TOKEN_TILE == 0
    assert (n_batch * (n_ctx + n_latent)) % PEER_TOKEN_TILE == 0

    xs = jnp.concatenate([ctx, x], axis=1)
    mod_rows = -(-(n_batch + 1) // SUBLANES) * SUBLANES
    cvec = jnp.concatenate([c, c_ctx[None], jnp.zeros((mod_rows - n_batch - 1, d), F32)], axis=0)
    mods = _ada_call(cvec, ada_w, ada_b).reshape(depth, mod_rows, 6, d)
    wk1_all, wk2_all = _wk_call(peer_wq, peer_keys1, peer_keys2)
    key_major = lambda w: w.reshape(depth, PEER_HEADS, N_KEYS, d).transpose(0, 2, 1, 3).reshape(w.shape)
    wk1_all, wk2_all = key_major(wk1_all), key_major(wk2_all)
    u_pk_all, vt_pk_all = _peer_tables_call(peer_u, peer_v)
    rope_tabs = _rope_tables(n_ctx, n_latent)
    zeros_d = jnp.zeros((d,), F32)
    ones_d = jnp.ones((d,), F32)

    for layer in range(depth):
        kind, slot = layer % 3, layer // 3
        mod = mods[layer]
        if kind == 0:
            yin = _attn_layer(xs, mod, norm_g[layer, 0], attn_wqkv[slot], attn_q_gain[slot], attn_k_gain[slot],
                              attn_sink[slot], rope_tabs, n_ctx, layer < depth - 1)
            post = (False, yin, xs, mod, attn_wo[slot], zeros_d, norm_g[layer, 1], ones_d, zeros_d)
        elif kind == 1:
            yin = _lru_layer(xs, mod, norm_g[layer, 0], lru_w_in[slot], lru_conv_w[slot], lru_conv_b[slot],
                             lru_w_a[slot], lru_b_a[slot], lru_w_x[slot], lru_b_x[slot], lru_lambda[slot], n_ctx)
            post = (False, yin, xs, mod, lru_w_out[slot], zeros_d, norm_g[layer, 1], ones_d, zeros_d)
        else:
            yin = _conf_layer(xs, mod, norm_g[layer, 0], conf_w_pw1[slot], conf_b_pw1[slot], conf_dw_w[slot],
                              conf_dw_b[slot], n_ctx)
            post = (True, yin, xs, mod, conf_w_pw2[slot], conf_b_pw2[slot], norm_g[layer, 1], conf_ln_g[slot],
                    conf_ln_b[slot])
        last = layer == depth - 1
        x1, h2t, r2, e2, beta, c1 = _post_call(*post, wk1_all, wk2_all, layer, last)
        x2 = _peer_call(h2t, r2, e2, beta, c1, u_pk_all, vt_pk_all, layer, x1, mod, n_batch, last)
        if not last:
            xs = x2.reshape(n_batch, n_ctx + n_latent, d)
    return x2.reshape(n_batch, n_latent, d)
```
